```python
import jax, jax.numpy as jnp
from jax import lax
import numpy as np

D_MODEL = 1024
BATCH = 16
SEQ = 256
DEPTH = 1
DEC_BATCH = 8
DEC_SEQ = 1024
PAST_LEN = 512

GRID_W = 64
CONV_WIDTH = 512
CONV_K = 3
NA_HEADS = 8
NA_HEAD_DIM = 64
NA_WIDTH = NA_HEADS * NA_HEAD_DIM
WIN_H = 8
WIN_W = 16
COL_BLOCK = 16
COL_BAND = 32
PEER_HEADS = 8
PEER_KEY_DIM = 256
N_KEYS = 128
N_EXPERTS = N_KEYS * N_KEYS
PEER_TOPK = 16
PEER_BLOCK = 128
LN_EPS = 1e-5
NEG_INF = -1e30
ALPHA = (2 * DEPTH) ** 0.25
BETA = (8 * DEPTH) ** -0.25
IN_COLS = 3 * CONV_WIDTH + 3 * NA_WIDTH + 2 * D_MODEL
IN_SPLITS = (CONV_WIDTH, 2 * CONV_WIDTH, 3 * CONV_WIDTH,
             3 * CONV_WIDTH + NA_WIDTH, 3 * CONV_WIDTH + 2 * NA_WIDTH,
             3 * CONV_WIDTH + 3 * NA_WIDTH, 3 * CONV_WIDTH + 3 * NA_WIDTH + D_MODEL)

kernel_name = "hybrid_flow_shortconv_natten_peer_step"


def layer_norm(x, g, b):
    xf = x.astype(jnp.float32)
    mu = jnp.mean(xf, axis=-1, keepdims=True)
    var = jnp.mean(jnp.square(xf - mu), axis=-1, keepdims=True)
    y = (xf - mu) * lax.rsqrt(var + LN_EPS)
    return y.astype(x.dtype) * g + b


def short_conv(xc, bg, cg, conv_w):
    h = cg * xc
    n = h.shape[1]
    pad = CONV_K // 2
    hp = jnp.pad(h, ((0, 0), (pad, pad), (0, 0)))
    y = conv_w[0] * hp[:, 0:n]
    for i in range(1, CONV_K):
        y = y + conv_w[i] * hp[:, i:i + n]
    return bg * y


def context_attention(q, k, v):
    s = jnp.einsum('bqhd,bkhd->bhqk', q, k).astype(jnp.float32) * (NA_HEAD_DIM ** -0.5)
    p = jax.nn.softmax(s, axis=-1).astype(v.dtype)
    return jnp.einsum('bhqk,bkhd->bqhd', p, v)


def neighbourhood_attention(q, k, v, k_ctx, v_ctx, rpb):
    b, n, h, dh = q.shape
    rows = n // GRID_W
    kh = min(WIN_H, rows)
    n_cb = GRID_W // COL_BLOCK
    r = np.arange(rows)
    row_start = np.clip(r - kh // 2, 0, rows - kh)
    row_idx = row_start[:, None] + np.arange(kh)[None, :]
    col = np.arange(GRID_W)
    col_start = np.clip(col - WIN_W // 2, 0, GRID_W - WIN_W).reshape(n_cb, COL_BLOCK)
    qcol = col.reshape(n_cb, COL_BLOCK)
    band_start = np.clip(np.arange(n_cb) * COL_BLOCK - WIN_W // 2, 0, GRID_W - COL_BAND)
    band_idx = band_start[:, None] + np.arange(COL_BAND)[None, :]
    in_win = ((band_idx[:, None, :] >= col_start[:, :, None]) &
              (band_idx[:, None, :] < col_start[:, :, None] + WIN_W))
    dr = row_idx - r[:, None] + (WIN_H - 1)
    dc = np.clip(band_idx[:, None, :] - qcol[:, :, None], -(WIN_W - 1), WIN_W - 1) + (WIN_W - 1)
    bias = rpb.astype(jnp.float32)[:, dr[:, None, None, :, None], dc[None, :, :, None, :]]
    bias = jnp.where(jnp.asarray(in_win)[None, None, :, :, None, :], bias, NEG_INF)

    qg = q.reshape(b, rows, n_cb, COL_BLOCK, h, dh)
    kg = k.reshape(b, rows, GRID_W, h, dh)
    vg = v.reshape(b, rows, GRID_W, h, dh)
    k_band = kg[:, row_idx][:, :, :, band_idx]
    v_band = vg[:, row_idx][:, :, :, band_idx]
    scale = NA_HEAD_DIM ** -0.5
    s_win = jnp.einsum('brjqhd,brijkhd->bhrjqik', qg, k_band).astype(jnp.float32) * scale + bias[None]
    s_ctx = jnp.einsum('brjqhd,bchd->bhrjqc', qg, k_ctx).astype(jnp.float32) * scale
    n_win = kh * COL_BAND
    s = jnp.concatenate([s_win.reshape(s_win.shape[:5] + (n_win,)), s_ctx], axis=-1)
    p = jax.nn.softmax(s, axis=-1).astype(v.dtype)
    p_win = p[..., :n_win].reshape(s_win.shape)
    p_ctx = p[..., n_win:]
    o = (jnp.einsum('bhrjqik,brijkhd->brjqhd', p_win, v_band) +
         jnp.einsum('bhrjqc,bchd->brjqhd', p_ctx, v_ctx))
    return o.reshape(b, n, h, dh)


def peer(x, w_query, sub_keys, expert_u, expert_v):
    b, n, d = x.shape
    t = b * n
    xt = x.reshape(t, d)
    q = (xt @ w_query).reshape(t, PEER_HEADS, 2, PEER_KEY_DIM // 2)
    s = jnp.einsum('thpd,hpkd->thpk', q, sub_keys).astype(jnp.float32)
    s_half, i_half = lax.top_k(s, PEER_TOPK)
    cand = s_half[:, :, 0, :, None] + s_half[:, :, 1, None, :]
    cand_idx = i_half[:, :, 0, :, None] * N_KEYS + i_half[:, :, 1, None, :]
    top_s, top_pos = lax.top_k(cand.reshape(t, PEER_HEADS, PEER_TOPK * PEER_TOPK), PEER_TOPK)
    experts = jnp.take_along_axis(cand_idx.reshape(t, PEER_HEADS, PEER_TOPK * PEER_TOPK), top_pos, axis=-1)
    gates = jax.nn.softmax(top_s, axis=-1).astype(x.dtype)
    hk = PEER_HEADS * PEER_TOPK
    nb = -(-t // PEER_BLOCK)
    pad = nb * PEER_BLOCK - t
    xb = jnp.pad(xt, ((0, pad), (0, 0))).reshape(nb, PEER_BLOCK, d)
    eb = jnp.pad(experts.reshape(t, hk), ((0, pad), (0, 0))).reshape(nb, PEER_BLOCK, hk)
    gb = jnp.pad(gates.reshape(t, hk), ((0, pad), (0, 0))).reshape(nb, PEER_BLOCK, hk)

    def apply_block(args):
        xs, es, gs = args
        u = expert_u[es]
        vv = expert_v[es]
        a = jax.nn.gelu(jnp.einsum('tkd,td->tk', u, xs), approximate=False)
        return jnp.einsum('tk,tkd->td', gs * a, vv)

    y = lax.map(apply_block, (xb, eb, gb)).reshape(nb * PEER_BLOCK, d)[:t]
    return y.reshape(b, n, d)


def layer(x, cond, k_ctx, v_ctx, w_ada, b_ada, w_in, conv_w, w_conv_out, w_na_out, rpb, w_o,
          ln1_g, ln1_b, w_query, sub_keys, expert_u, expert_v, ln2_g, ln2_b):
    b, n, _ = x.shape
    mods = (jax.nn.silu(cond) @ w_ada + b_ada)[:, None, :]
    sh1, sc1, ga1, sh2, sc2, ga2 = jnp.split(mods, 6, axis=-1)
    u = x * (1 + sc1) + sh1
    xc, bg, cg, q, k, v, g_conv, g_na = jnp.split(u @ w_in, IN_SPLITS, axis=-1)
    y_conv = short_conv(xc, bg, cg, conv_w)
    q = q.reshape(b, n, NA_HEADS, NA_HEAD_DIM)
    k = k.reshape(b, n, NA_HEADS, NA_HEAD_DIM)
    v = v.reshape(b, n, NA_HEADS, NA_HEAD_DIM)
    if k_ctx is None:
        y_na = context_attention(q, k, v)
    else:
        y_na = neighbourhood_attention(q, k, v, k_ctx, v_ctx, rpb)
    y_na = y_na.reshape(b, n, NA_WIDTH)
    merged = (jax.nn.sigmoid(g_conv) * (y_conv @ w_conv_out) +
              jax.nn.sigmoid(g_na) * (y_na @ w_na_out))
    x = layer_norm(ALPHA * x + ga1 * (merged @ w_o), ln1_g, ln1_b)
    u2 = x * (1 + sc2) + sh2
    x = layer_norm(ALPHA * x + ga2 * peer(u2, w_query, sub_keys, expert_u, expert_v), ln2_g, ln2_b)
    return x, k, v


def setup_inputs(seed: int = 0) -> dict:
    key = jax.random.key(seed)
    ks = jax.random.split(key, 24)

    def nrm(k, shape, s):
        return jax.random.normal(k, shape, jnp.float32) * s

    cache_shape = (DEC_BATCH, DEPTH, PAST_LEN, NA_HEADS, NA_HEAD_DIM)
    return {
        'x_prompt': nrm(ks[0], (BATCH, SEQ, D_MODEL), 1.0),
        'x_sample': nrm(ks[1], (DEC_BATCH, DEC_SEQ, D_MODEL), 1.0),
        'cache_k': nrm(ks[2], cache_shape, 1.0),
        'cache_v': nrm(ks[3], cache_shape, 1.0),
        'c': nrm(ks[4], (DEC_BATCH, D_MODEL), 1.0),
        'c_ctx': nrm(ks[5], (D_MODEL,), 1.0),
        'w_ada': nrm(ks[6], (DEPTH, D_MODEL, 6 * D_MODEL), 0.5 * D_MODEL ** -0.5),
        'b_ada': nrm(ks[7], (DEPTH, 6 * D_MODEL), 0.02),
        'w_in': nrm(ks[8], (DEPTH, D_MODEL, IN_COLS), D_MODEL ** -0.5),
        'conv_w': nrm(ks[9], (DEPTH, CONV_K, CONV_WIDTH), CONV_K ** -0.5),
        'w_conv_out': nrm(ks[10], (DEPTH, CONV_WIDTH, D_MODEL), CONV_WIDTH ** -0.5),
        'w_na_out': nrm(ks[11], (DEPTH, NA_WIDTH, D_MODEL), NA_WIDTH ** -0.5),
        'rpb': nrm(ks[12], (DEPTH, NA_HEADS, 2 * WIN_H - 1, 2 * WIN_W - 1), 0.1),
        'w_o': nrm(ks[13], (DEPTH, D_MODEL, D_MODEL), BETA * D_MODEL ** -0.5),
        'ln1_g': 1.0 + nrm(ks[14], (DEPTH, D_MODEL), 0.02),
        'ln1_b': nrm(ks[15], (DEPTH, D_MODEL), 0.02),
        'w_query': nrm(ks[16], (DEPTH, D_MODEL, PEER_HEADS * PEER_KEY_DIM), D_MODEL ** -0.5),
        'sub_keys': nrm(ks[17], (DEPTH, PEER_HEADS, 2, N_KEYS, PEER_KEY_DIM // 2), (PEER_KEY_DIM // 2) ** -0.5),
        'expert_u': nrm(ks[18], (DEPTH, N_EXPERTS, D_MODEL), D_MODEL ** -0.5),
        'expert_v': nrm(ks[19], (DEPTH, N_EXPERTS, D_MODEL), BETA),
        'ln2_g': 1.0 + nrm(ks[20], (DEPTH, D_MODEL), 0.02),
        'ln2_b': nrm(ks[21], (DEPTH, D_MODEL), 0.02),
    }


def reference(x_prompt, x_sample, cache_k, cache_v, c, c_ctx, w_ada, b_ada, w_in, conv_w,
              w_conv_out, w_na_out, rpb, w_o, ln1_g, ln1_b, w_query, sub_keys, expert_u,
              expert_v, ln2_g, ln2_b):
    h_ctx = x_prompt
    h_lat = x_sample
    new_k, new_v = [], []
    for l in range(DEPTH):
        params = (w_ada[l], b_ada[l], w_in[l], conv_w[l], w_conv_out[l], w_na_out[l], rpb[l],
                  w_o[l], ln1_g[l], ln1_b[l], w_query[l], sub_keys[l], expert_u[l], expert_v[l],
                  ln2_g[l], ln2_b[l])
        h_ctx, k_l, v_l = layer(h_ctx, c_ctx[None, :], None, None, *params)
        new_k.append(k_l)
        new_v.append(v_l)
        h_lat, _, _ = layer(h_lat, c, cache_k[:, l], cache_v[:, l], *params)
    new_cache_k = jnp.stack(new_k, axis=1)
    new_cache_v = jnp.stack(new_v, axis=1)
    return (h_ctx, h_lat, new_cache_k, new_cache_v)
```

```python
import functools

import numpy as np
import jax
import jax.numpy as jnp
from jax import lax
from jax.experimental import pallas as pl
from jax.experimental.pallas import tpu as pltpu

F32 = jnp.float32
BF16 = jnp.bfloat16

D_MODEL = 1024
BATCH = 16
SEQ = 256
DEC_BATCH = 8
DEC_SEQ = 1024
PAST_LEN = 512
GRID_W = 64
GRID_ROWS = DEC_SEQ // GRID_W
CONV_WIDTH = 512
CONV_K = 3
NA_HEADS = 8
NA_HEAD_DIM = 64
NA_WIDTH = NA_HEADS * NA_HEAD_DIM
WIN_H = 8
WIN_W = 16
PEER_HEADS = 8
PEER_KEY_DIM = 256
N_KEYS = 128
N_EXPERTS = N_KEYS * N_KEYS
PEER_TOPK = 16
LN_EPS = 1e-5
NEG_INF = -1e30
DEPTH = 1
ALPHA = (2 * DEPTH) ** 0.25
IN_COLS = 3 * CONV_WIDTH + 3 * NA_WIDTH + 2 * D_MODEL

N_CTX = BATCH * SEQ
N_LAT = DEC_BATCH * DEC_SEQ
N_TOK = N_CTX + N_LAT
N_COND = 16

ROW_TILE = 256
CTX_TILES = N_CTX // ROW_TILE
LAT_TILES_PER_SEQ = DEC_SEQ // ROW_TILE
N_TILES = N_TOK // ROW_TILE

PEER_TOK = 512
PEER_LANE = 256
PEER_EB = 1024
PEER_ROWS_PER_EB = PEER_EB // N_KEYS
SEL_TOK = 256

VMEM_LIMIT = 56 * 1024 * 1024

N_RANK = PEER_TOPK + 1
STAIR = [(a, b) for a in range(N_RANK) for b in range(N_RANK) if (a + 1) * (b + 1) <= N_RANK]


def _cparams(sem):
    return pltpu.CompilerParams(dimension_semantics=sem, vmem_limit_bytes=VMEM_LIMIT)


def _tile_mod_row(i):
    return jnp.where(i < CTX_TILES, 0, 1 + (i - CTX_TILES) // LAT_TILES_PER_SEQ)


def _mods_kernel(cond_ref, w_ref, b_ref, o_ref):
    cnd = cond_ref[...]
    act = cnd * jax.nn.sigmoid(cnd)
    o_ref[...] = jnp.dot(act, w_ref[...], precision=lax.Precision.HIGHEST,
                         preferred_element_type=F32) + b_ref[...]


def _mods(cond, w_ada, b_ada):
    cols = 6 * D_MODEL
    blk = 1536
    return pl.pallas_call(
        _mods_kernel,
        grid=(cols // blk,),
        in_specs=[pl.BlockSpec((N_COND, D_MODEL), lambda j: (0, 0)),
                  pl.BlockSpec((D_MODEL, blk), lambda j: (0, j)),
                  pl.BlockSpec((1, blk), lambda j: (0, j))],
        out_specs=pl.BlockSpec((N_COND, blk), lambda j: (0, j)),
        out_shape=jax.ShapeDtypeStruct((N_COND, cols), F32),
        compiler_params=_cparams(("arbitrary",)),
        name="mods",
    )(cond, w_ada, b_ada.reshape(1, cols))


def _inproj_kernel(x_ref, mod_ref, w_ref, h_ref, bg_ref, q_ref, k_ref, v_ref, gc_ref, gn_ref):
    sh1 = mod_ref[0, :, 0:D_MODEL]
    sc1 = mod_ref[0, :, D_MODEL:2 * D_MODEL]
    u = (x_ref[...] * (1 + sc1) + sh1).astype(BF16)

    def proj(lo, hi):
        return jnp.dot(u, w_ref[:, lo:hi], preferred_element_type=F32)

    c = CONV_WIDTH
    a = NA_WIDTH
    xc = proj(0, c)
    cg = proj(2 * c, 3 * c)
    h_ref[...] = cg * xc
    bg_ref[...] = proj(c, 2 * c)
    o = 3 * c
    q_ref[...] = (proj(o, o + a) * (NA_HEAD_DIM ** -0.5)).astype(BF16)
    k_ref[...] = proj(o + a, o + 2 * a)
    v_ref[...] = proj(o + 2 * a, o + 3 * a)
    o = o + 3 * a
    gc_ref[...] = jax.nn.sigmoid(proj(o, o + D_MODEL))
    gn_ref[...] = jax.nn.sigmoid(proj(o + D_MODEL, o + 2 * D_MODEL))


def _inproj(x_all, mods3, w_in_bf):
    row = lambda w: pl.BlockSpec((ROW_TILE, w), lambda i: (i, 0))
    return pl.pallas_call(
        _inproj_kernel,
        grid=(N_TILES,),
        in_specs=[row(D_MODEL),
                  pl.BlockSpec((1, 1, 6 * D_MODEL), lambda i: (_tile_mod_row(i), 0, 0)),
                  pl.BlockSpec((D_MODEL, IN_COLS), lambda i: (0, 0))],
        out_specs=[row(CONV_WIDTH), row(CONV_WIDTH), row(NA_WIDTH), row(NA_WIDTH), row(NA_WIDTH),
                   row(D_MODEL), row(D_MODEL)],
        out_shape=[jax.ShapeDtypeStruct((N_TOK, CONV_WIDTH), F32),
                   jax.ShapeDtypeStruct((N_TOK, CONV_WIDTH), F32),
                   jax.ShapeDtypeStruct((N_TOK, NA_WIDTH), BF16),
                   jax.ShapeDtypeStruct((N_TOK, NA_WIDTH), F32),
                   jax.ShapeDtypeStruct((N_TOK, NA_WIDTH), F32),
                   jax.ShapeDtypeStruct((N_TOK, D_MODEL), F32),
                   jax.ShapeDtypeStruct((N_TOK, D_MODEL), F32)],
        compiler_params=_cparams(("parallel",)),
        name="inproj",
    )(x_all, mods3, w_in_bf)


def _head_pair_attention(q2, kv_list, bias_list):
    lane = lax.broadcasted_iota(jnp.int32, q2.shape, 1)
    outs = []
    for hh in range(2):
        in_head = (lane >= hh * NA_HEAD_DIM) & (lane < (hh + 1) * NA_HEAD_DIM)
        qh = jnp.where(in_head, q2, jnp.zeros_like(q2))
        scores = []
        for (k2, _), bias in zip(kv_list, bias_list):
            s = lax.dot_general(qh, k2, (((1,), (1,)), ((), ())), preferred_element_type=F32)
            if bias is not None:
                s = s + bias(hh)
            scores.append(s)
        m = scores[0].max(axis=-1, keepdims=True)
        for s in scores[1:]:
            m = jnp.maximum(m, s.max(axis=-1, keepdims=True))
        acc = None
        den = None
        for s, (_, v2) in zip(scores, kv_list):
            p = jnp.exp(s - m)
            d = p.sum(axis=-1, keepdims=True)
            o = jnp.dot(p.astype(BF16), v2, preferred_element_type=F32)
            acc = o if acc is None else acc + o
            den = d if den is None else den + d
        outs.append(acc / den)
    return jnp.where(lane < NA_HEAD_DIM, outs[0], outs[1])


def _ctx_attn_kernel(q_ref, k_ref, v_ref, o_ref):
    k2 = k_ref[...].astype(BF16)
    v2 = v_ref[...].astype(BF16)
    o_ref[...] = _head_pair_attention(q_ref[...], [(k2, v2)], [None]).astype(BF16)


def _ctx_attn(q, k, v):
    spec = pl.BlockSpec((SEQ, 128), lambda b, p: (b, p))
    return pl.pallas_call(
        _ctx_attn_kernel,
        grid=(BATCH, NA_WIDTH // 128),
        in_specs=[spec, spec, spec],
        out_specs=spec,
        out_shape=jax.ShapeDtypeStruct((N_CTX, NA_WIDTH), BF16),
        compiler_params=_cparams(("parallel", "parallel")),
        name="ctx_attn",
    )(q, k, v)


def _lat_attn_kernel(q_ref, k_ref, v_ref, kc_ref, vc_ref, bias_ref, o_ref):
    kw = k_ref[...].astype(BF16)
    vw = v_ref[...].astype(BF16)
    kc = kc_ref[0].astype(BF16)
    vc = vc_ref[0].astype(BF16)
    o = _head_pair_attention(q_ref[...], [(kw, vw), (kc, vc)], [lambda hh: bias_ref[hh], None])
    o_ref[...] = o.astype(BF16)


LAT_QB = 256


def _lat_attn(q, k, v, k_ctx, v_ctx, bias):
    n_qb = DEC_SEQ // LAT_QB
    q_off = N_CTX // LAT_QB
    kv_off = N_CTX // DEC_SEQ
    return pl.pallas_call(
        _lat_attn_kernel,
        grid=(n_qb, NA_WIDTH // 128, DEC_BATCH),
        in_specs=[pl.BlockSpec((LAT_QB, 128), lambda qb, p, b: (q_off + b * n_qb + qb, p)),
                  pl.BlockSpec((DEC_SEQ, 128), lambda qb, p, b: (kv_off + b, p)),
                  pl.BlockSpec((DEC_SEQ, 128), lambda qb, p, b: (kv_off + b, p)),
                  pl.BlockSpec((1, PAST_LEN, 128), lambda qb, p, b: (b, 0, p)),
                  pl.BlockSpec((1, PAST_LEN, 128), lambda qb, p, b: (b, 0, p)),
                  pl.BlockSpec((2, LAT_QB, DEC_SEQ), lambda qb, p, b: (p, qb, 0))],
        out_specs=pl.BlockSpec((LAT_QB, 128), lambda qb, p, b: (b * n_qb + qb, p)),
        out_shape=jax.ShapeDtypeStruct((N_LAT, NA_WIDTH), BF16),
        compiler_params=_cparams(("parallel", "parallel", "parallel")),
        name="lat_attn",
    )(q, k, v, k_ctx, v_ctx, bias)


def _window_bias(rpb):
    pos = np.arange(DEC_SEQ)
    r, c = pos // GRID_W, pos % GRID_W
    kh = min(WIN_H, GRID_ROWS)
    row_start = np.clip(r - kh // 2, 0, GRID_ROWS - kh)
    col_start = np.clip(c - WIN_W // 2, 0, GRID_W - WIN_W)
    in_win = ((r[None, :] >= row_start[:, None]) & (r[None, :] < row_start[:, None] + kh) &
              (c[None, :] >= col_start[:, None]) & (c[None, :] < col_start[:, None] + WIN_W))
    dr = np.clip(r[None, :] - r[:, None] + (WIN_H - 1), 0, 2 * WIN_H - 2)
    dc = np.clip(c[None, :] - c[:, None], -(WIN_W - 1), WIN_W - 1) + (WIN_W - 1)
    bias = rpb.astype(F32)[:, dr, dc]
    return jnp.where(jnp.asarray(in_win)[None], bias, NEG_INF)


def _layer_norm(x, g, b):
    mu = jnp.mean(x, axis=-1, keepdims=True)
    xc = x - mu
    var = jnp.mean(xc * xc, axis=-1, keepdims=True)
    return xc * lax.rsqrt(var + LN_EPS) * g + b


def _merge_kernel(x_ref, mod_ref, h_ref, hp_ref, hn_ref, bg_ref, yna_ref, gc_ref, gn_ref,
                  cw_ref, wco_ref, wno_ref, wo_ref, g1_ref, b1_ref, wqt_ref, sk_ref,
                  x1_ref, u2_ref, st_ref):
    i = pl.program_id(0)
    pos = (i - CTX_TILES) % LAT_TILES_PER_SEQ
    is_ctx = i < CTX_TILES
    has_prev = jnp.where(is_ctx | (pos == 0), 0.0, 1.0).astype(F32)
    has_next = jnp.where(is_ctx | (pos == LAT_TILES_PER_SEQ - 1), 0.0, 1.0).astype(F32)

    h = h_ref[...]
    rows = lax.broadcasted_iota(jnp.int32, h.shape, 0)
    prev_row = hp_ref[7:8, :] * has_prev
    next_row = hn_ref[0:1, :] * has_next
    h_prev = jnp.where(rows == 0, prev_row, pltpu.roll(h, 1, 0))
    h_next = jnp.where(rows == ROW_TILE - 1, next_row, pltpu.roll(h, ROW_TILE - 1, 0))
    y = cw_ref[0:1, :] * h_prev
    y = y + cw_ref[1:2, :] * h
    y = y + cw_ref[2:3, :] * h_next
    y_conv = (bg_ref[...] * y).astype(BF16)

    merged = (gc_ref[...] * jnp.dot(y_conv, wco_ref[...], preferred_element_type=F32) +
              gn_ref[...] * jnp.dot(yna_ref[...], wno_ref[...], preferred_element_type=F32))
    z = jnp.dot(merged.astype(BF16), wo_ref[...], preferred_element_type=F32)

    ga1 = mod_ref[0, :, 2 * D_MODEL:3 * D_MODEL]
    sh2 = mod_ref[0, :, 3 * D_MODEL:4 * D_MODEL]
    sc2 = mod_ref[0, :, 4 * D_MODEL:5 * D_MODEL]
    x1 = _layer_norm(ALPHA * x_ref[...] + ga1 * z, g1_ref[...], b1_ref[...])
    x1_ref[...] = x1
    u2 = (x1 * (1 + sc2) + sh2).astype(BF16)
    u2_ref[...] = u2

    qt = lax.dot_general(wqt_ref[...], u2, (((1,), (1,)), ((), ())), preferred_element_type=F32)
    half = PEER_KEY_DIM // 2
    for hp in range(2 * PEER_HEADS):
        blk = qt[hp * half:(hp + 1) * half, :].astype(BF16)
        st_ref[hp * N_KEYS:(hp + 1) * N_KEYS, :] = jnp.dot(sk_ref[hp], blk, preferred_element_type=F32)


def _merge(x_all, mods3, h, bg, y_na, gc, gn, conv_w, wco, wno, wo, g1, b1, wqt, sk):
    row = lambda w: pl.BlockSpec((ROW_TILE, w), lambda i: (i, 0))
    full = lambda *s: pl.BlockSpec(s, lambda i: (0,) * len(s))
    eights = ROW_TILE // 8
    n8 = N_TOK // 8
    return pl.pallas_call(
        _merge_kernel,
        grid=(N_TILES,),
        in_specs=[row(D_MODEL),
                  pl.BlockSpec((1, 1, 6 * D_MODEL), lambda i: (_tile_mod_row(i), 0, 0)),
                  row(CONV_WIDTH),
                  pl.BlockSpec((8, CONV_WIDTH), lambda i: (jnp.maximum(i * eights - 1, 0), 0)),
                  pl.BlockSpec((8, CONV_WIDTH), lambda i: (jnp.minimum((i + 1) * eights, n8 - 1), 0)),
                  row(CONV_WIDTH), row(NA_WIDTH), row(D_MODEL), row(D_MODEL),
                  full(CONV_K, CONV_WIDTH), full(CONV_WIDTH, D_MODEL), full(NA_WIDTH, D_MODEL),
                  full(D_MODEL, D_MODEL), full(1, D_MODEL), full(1, D_MODEL),
                  full(PEER_HEADS * PEER_KEY_DIM, D_MODEL),
                  full(2 * PEER_HEADS, N_KEYS, PEER_KEY_DIM // 2)],
        out_specs=[row(D_MODEL), row(D_MODEL),
                   pl.BlockSpec((2 * PEER_HEADS * N_KEYS, ROW_TILE), lambda i: (0, i))],
        out_shape=[jax.ShapeDtypeStruct((N_TOK, D_MODEL), F32),
                   jax.ShapeDtypeStruct((N_TOK, D_MODEL), BF16),
                   jax.ShapeDtypeStruct((2 * PEER_HEADS * N_KEYS, N_TOK), F32)],
        compiler_params=_cparams(("parallel",)),
        name="merge",
    )(x_all, mods3, h, h, h, bg, y_na, gc, gn, conv_w, wco, wno, wo, g1, b1, wqt, sk)


def _sublane_max(x):
    x = jnp.maximum(x, pltpu.roll(x, 4, 0))
    x = jnp.maximum(x, pltpu.roll(x, 2, 0))
    return jnp.maximum(x, pltpu.roll(x, 1, 0))


def _select_kernel(st_ref, tau_ref, m0_ref, m1_ref, zinv_ref, top_ref):
    n_chunks = N_KEYS // 8

    def one_list(hp, carry):
        base = pl.multiple_of(hp * N_KEYS, N_KEYS)
        cur = [st_ref[pl.ds(base + 8 * c, 8), :] for c in range(n_chunks)]
        for r in range(N_RANK):
            m = cur[0]
            for c in range(1, n_chunks):
                m = jnp.maximum(m, cur[c])
            m = _sublane_max(m)
            top_ref[hp, r] = m
            if r + 1 < N_RANK:
                cur = [jnp.where(x == m, -jnp.inf, x) for x in cur]
        return carry

    lax.fori_loop(0, 2 * PEER_HEADS, one_list, 0)

    sub = lax.broadcasted_iota(jnp.int32, (PEER_HEADS, SEL_TOK), 0)

    def packed(p, r):
        out = top_ref[p, r]
        for hd in range(1, PEER_HEADS):
            out = jnp.where(sub == hd, top_ref[2 * hd + p, r], out)
        return out

    a = [packed(0, r) for r in range(N_RANK)]
    b = [packed(1, r) for r in range(N_RANK)]
    cand = [a[r0] + b[r1] for (r0, r1) in STAIR]

    def next_below(t):
        nxt = None
        for x in cand[1:]:
            y = jnp.where(x < t, x, -jnp.inf)
            nxt = y if nxt is None else jnp.maximum(nxt, y)
        return nxt

    top = cand[0]
    t = top
    c16 = jnp.full_like(top, -jnp.inf)
    c17 = jnp.full_like(top, -jnp.inf)
    for step in range(N_RANK):
        if step > 0:
            t = next_below(t)
        cnt = jnp.zeros_like(top)
        for x in cand:
            cnt = cnt + jnp.where(x >= t, 1.0, 0.0)
        c16 = jnp.maximum(c16, jnp.where(cnt >= PEER_TOPK, t, -jnp.inf))
        c17 = jnp.maximum(c17, jnp.where(cnt >= N_RANK, t, -jnp.inf))
    t = 0.5 * (c16 + c17)
    z = jnp.ones_like(top)
    for x in cand[1:]:
        z = z + jnp.where(x >= t, jnp.exp(x - top), 0.0)
    tau_ref[...] = t
    m0_ref[...] = a[0]
    m1_ref[...] = b[0]
    zinv_ref[...] = 1.0 / z


def _select(st):
    n_tok = st.shape[1]
    stat = pl.BlockSpec((PEER_HEADS, SEL_TOK), lambda i: (0, i))
    shp = jax.ShapeDtypeStruct((PEER_HEADS, n_tok), F32)
    return pl.pallas_call(
        _select_kernel,
        grid=(n_tok // SEL_TOK,),
        in_specs=[pl.BlockSpec((2 * PEER_HEADS * N_KEYS, SEL_TOK), lambda i: (0, i))],
        out_specs=[stat, stat, stat, stat],
        out_shape=[shp, shp, shp, shp],
        scratch_shapes=[pltpu.VMEM((2 * PEER_HEADS, N_RANK, 8, SEL_TOK), F32)],
        compiler_params=_cparams(("parallel",)),
        name="select",
    )(st)


def _gelu(x):
    return 0.5 * x * (1.0 + lax.erf(x * np.float32(np.sqrt(0.5))))


def _peer_kernel(u2_ref, st_ref, tau_ref, m0_ref, m1_ref, zinv_ref, u_ref, vt_ref,
                 x1_ref, mod_ref, g2_ref, b2_ref, o_ref,
                 thr_ref, e0_ref, e1_ref, s_ref, p_ref, acc_ref):
    eb = pl.program_id(1)

    @pl.when(eb == 0)
    def _prologue():
        for hd in range(PEER_HEADS):
            s0 = st_ref[(2 * hd) * N_KEYS:(2 * hd + 1) * N_KEYS, :]
            s1 = st_ref[(2 * hd + 1) * N_KEYS:(2 * hd + 2) * N_KEYS, :]
            rows = slice(hd * N_KEYS, (hd + 1) * N_KEYS)
            thr_ref[rows, :] = tau_ref[hd:hd + 1, :] - s0
            e0_ref[rows, :] = jnp.exp(s0 - m0_ref[hd:hd + 1, :]) * zinv_ref[hd:hd + 1, :]
            e1_ref[rows, :] = jnp.exp(s1 - m1_ref[hd:hd + 1, :])
        acc_ref[...] = jnp.zeros_like(acc_ref)

    s_ref[...] = lax.dot_general(u_ref[...], u2_ref[...], (((1,), (1,)), ((), ())),
                                 preferred_element_type=F32)

    def gate_rows(r, carry):
        row0 = pl.multiple_of(r * N_KEYS, N_KEYS)
        key0 = eb * PEER_ROWS_PER_EB + r
        for lane0 in range(0, PEER_TOK, PEER_LANE):
            lanes = slice(lane0, lane0 + PEER_LANE)
            w = jnp.zeros((N_KEYS, PEER_LANE), F32)
            for hd in range(PEER_HEADS):
                s1 = st_ref[(2 * hd + 1) * N_KEYS:(2 * hd + 2) * N_KEYS, lanes]
                e1 = e1_ref[hd * N_KEYS:(hd + 1) * N_KEYS, lanes]
                thr = thr_ref[pl.ds(hd * N_KEYS + key0, 1), lanes]
                e0 = e0_ref[pl.ds(hd * N_KEYS + key0, 1), lanes]
                w = w + jnp.where(s1 >= thr, e1, 0.0) * e0
            g = _gelu(s_ref[pl.ds(row0, N_KEYS), lanes])
            p_ref[pl.ds(row0, N_KEYS), lanes] = (w * g).astype(BF16)
        return carry

    lax.fori_loop(0, PEER_ROWS_PER_EB, gate_rows, 0)

    acc_ref[...] += jnp.dot(vt_ref[...], p_ref[...], preferred_element_type=F32)

    @pl.when(eb == pl.num_programs(1) - 1)
    def _epilogue():
        y = acc_ref[...].T
        ga2 = mod_ref[0, :, 5 * D_MODEL:6 * D_MODEL]
        o_ref[...] = _layer_norm(ALPHA * x1_ref[...] + ga2 * y, g2_ref[...], b2_ref[...])


def _peer(u2, st, tau, m0, m1, zinv, u_bf, vt_bf, x1, mods3, g2, b2):
    n_tok = u2.shape[0]
    tiles_per_seq = DEC_SEQ // PEER_TOK
    ctx_tiles = N_CTX // PEER_TOK
    mod_row = lambda t: jnp.where(t < ctx_tiles, 0, 1 + (t - ctx_tiles) // tiles_per_seq)
    stat = pl.BlockSpec((PEER_HEADS, PEER_TOK), lambda t, e: (0, t))
    return pl.pallas_call(
        _peer_kernel,
        grid=(n_tok // PEER_TOK, N_EXPERTS // PEER_EB),
        in_specs=[pl.BlockSpec((PEER_TOK, D_MODEL), lambda t, e: (t, 0)),
                  pl.BlockSpec((2 * PEER_HEADS * N_KEYS, PEER_TOK), lambda t, e: (0, t)),
                  stat, stat, stat, stat,
                  pl.BlockSpec((PEER_EB, D_MODEL), lambda t, e: (e, 0)),
                  pl.BlockSpec((D_MODEL, PEER_EB), lambda t, e: (0, e)),
                  pl.BlockSpec((PEER_TOK, D_MODEL), lambda t, e: (t, 0)),
                  pl.BlockSpec((1, 1, 6 * D_MODEL), lambda t, e: (mod_row(t), 0, 0)),
                  pl.BlockSpec((1, D_MODEL), lambda t, e: (0, 0)),
                  pl.BlockSpec((1, D_MODEL), lambda t, e: (0, 0))],
        out_specs=pl.BlockSpec((PEER_TOK, D_MODEL), lambda t, e: (t, 0)),
        out_shape=jax.ShapeDtypeStruct((n_tok, D_MODEL), F32),
        scratch_shapes=[pltpu.VMEM((PEER_HEADS * N_KEYS, PEER_TOK), F32),
                        pltpu.VMEM((PEER_HEADS * N_KEYS, PEER_TOK), F32),
                        pltpu.VMEM((PEER_HEADS * N_KEYS, PEER_TOK), F32),
                        pltpu.VMEM((PEER_EB, PEER_TOK), F32),
                        pltpu.VMEM((PEER_EB, PEER_TOK), BF16),
                        pltpu.VMEM((D_MODEL, PEER_TOK), F32)],
        compiler_params=_cparams(("parallel", "arbitrary")),
        name="peer",
    )(u2, st, tau, m0, m1, zinv, u_bf, vt_bf, x1, mods3, g2, b2)


def kernel(x_prompt, x_sample, cache_k, cache_v, c, c_ctx, w_ada, b_ada, w_in, conv_w, w_conv_out,
           w_na_out, rpb, w_o, ln1_g, ln1_b, w_query, sub_keys, expert_u, expert_v, ln2_g, ln2_b):
    l = 0
    x_all = jnp.concatenate([x_prompt.reshape(N_CTX, D_MODEL), x_sample.reshape(N_LAT, D_MODEL)], axis=0)
    cond = jnp.concatenate([c_ctx[None, :], c, jnp.zeros((N_COND - 1 - DEC_BATCH, D_MODEL), F32)], axis=0)

    mods = _mods(cond, w_ada[l], b_ada[l])
    mods3 = mods.reshape(N_COND, 1, 6 * D_MODEL)

    h, bg, q, k, v, gc, gn = _inproj(x_all, mods3, w_in[l].astype(BF16))

    y_ctx = _ctx_attn(q, k, v)
    k_ctx = cache_k[:, l].reshape(DEC_BATCH, PAST_LEN, NA_WIDTH)
    v_ctx = cache_v[:, l].reshape(DEC_BATCH, PAST_LEN, NA_WIDTH)
    y_lat = _lat_attn(q, k, v, k_ctx, v_ctx, _window_bias(rpb[l]))
    y_na = jnp.concatenate([y_ctx, y_lat], axis=0)

    wqt = w_query[l].T.astype(BF16)
    sk = sub_keys[l].reshape(2 * PEER_HEADS, N_KEYS, PEER_KEY_DIM // 2).astype(BF16)
    x1, u2, st = _merge(x_all, mods3, h, bg, y_na, gc, gn, conv_w[l],
                        w_conv_out[l].astype(BF16), w_na_out[l].astype(BF16), w_o[l].astype(BF16),
                        ln1_g[l].reshape(1, D_MODEL), ln1_b[l].reshape(1, D_MODEL), wqt, sk)

    tau, m0, m1, zinv = _select(st)
    x2 = _peer(u2, st, tau, m0, m1, zinv, expert_u[l].astype(BF16), expert_v[l].T.astype(BF16),
               x1, mods3, ln2_g[l].reshape(1, D_MODEL), ln2_b[l].reshape(1, D_MODEL))

    y_prompt = x2[:N_CTX].reshape(BATCH, SEQ, D_MODEL)
    y_sample = x2[N_CTX:].reshape(DEC_BATCH, DEC_SEQ, D_MODEL)
    new_k = k[:N_CTX].reshape(BATCH, 1, SEQ, NA_HEADS, NA_HEAD_DIM)
    new_v = v[:N_CTX].reshape(BATCH, 1, SEQ, NA_HEADS, NA_HEAD_DIM)
    return (y_prompt, y_sample, new_k, new_v)
```

```python
import functools

import numpy as np
import jax
import jax.numpy as jnp
from jax import lax
from jax.experimental import pallas as pl
from jax.experimental.pallas import tpu as pltpu

F32 = jnp.float32
BF16 = jnp.bfloat16

D_MODEL = 1024
BATCH = 16
SEQ = 256
DEC_BATCH = 8
DEC_SEQ = 1024
PAST_LEN = 512
GRID_W = 64
GRID_ROWS = DEC_SEQ // GRID_W
CONV_WIDTH = 512
CONV_K = 3
NA_HEADS = 8
NA_HEAD_DIM = 64
NA_WIDTH = NA_HEADS * NA_HEAD_DIM
WIN_H = 8
WIN_W = 16
PEER_HEADS = 8
PEER_KEY_DIM = 256
N_KEYS = 128
N_EXPERTS = N_KEYS * N_KEYS
PEER_TOPK = 16
LN_EPS = 1e-5
NEG_INF = -1e30
DEPTH = 1
ALPHA = (2 * DEPTH) ** 0.25
IN_COLS = 3 * CONV_WIDTH + 3 * NA_WIDTH + 2 * D_MODEL

N_CTX = BATCH * SEQ
N_LAT = DEC_BATCH * DEC_SEQ
N_TOK = N_CTX + N_LAT
N_COND = 16

ROW_TILE = 256
CTX_TILES = N_CTX // ROW_TILE
LAT_TILES_PER_SEQ = DEC_SEQ // ROW_TILE
N_TILES = N_TOK // ROW_TILE

PEER_TOK = 512
PEER_LANE = 256
PEER_EB = 1024
PEER_ROWS_PER_EB = PEER_EB // N_KEYS
SEL_TOK = 256

VMEM_LIMIT = 56 * 1024 * 1024

N_RANK = PEER_TOPK + 1
STAIR = [(a, b) for a in range(N_RANK) for b in range(N_RANK) if (a + 1) * (b + 1) <= N_RANK]


def _cparams(sem):
    return pltpu.CompilerParams(dimension_semantics=sem, vmem_limit_bytes=VMEM_LIMIT)


def _tile_mod_row(i):
    return jnp.where(i < CTX_TILES, 0, 1 + (i - CTX_TILES) // LAT_TILES_PER_SEQ)


def _mods_kernel(cond_ref, w_ref, b_ref, o_ref):
    cnd = cond_ref[...]
    act = cnd * jax.nn.sigmoid(cnd)
    o_ref[...] = jnp.dot(act, w_ref[...], precision=lax.Precision.HIGHEST,
                         preferred_element_type=F32) + b_ref[...]


def _mods(cond, w_ada, b_ada):
    cols = 6 * D_MODEL
    blk = 1536
    return pl.pallas_call(
        _mods_kernel,
        grid=(cols // blk,),
        in_specs=[pl.BlockSpec((N_COND, D_MODEL), lambda j: (0, 0)),
                  pl.BlockSpec((D_MODEL, blk), lambda j: (0, j)),
                  pl.BlockSpec((1, blk), lambda j: (0, j))],
        out_specs=pl.BlockSpec((N_COND, blk), lambda j: (0, j)),
        out_shape=jax.ShapeDtypeStruct((N_COND, cols), F32),
        compiler_params=_cparams(("arbitrary",)),
        name="mods",
    )(cond, w_ada, b_ada.reshape(1, cols))


def _inproj_kernel(x_ref, mod_ref, w_ref, h_ref, bg_ref, q_ref, k_ref, v_ref, gc_ref, gn_ref):
    sh1 = mod_ref[0, :, 0:D_MODEL]
    sc1 = mod_ref[0, :, D_MODEL:2 * D_MODEL]
    u = (x_ref[...] * (1 + sc1) + sh1).astype(BF16)

    def proj(lo, hi):
        return jnp.dot(u, w_ref[:, lo:hi], preferred_element_type=F32)

    c = CONV_WIDTH
    a = NA_WIDTH
    xc = proj(0, c)
    cg = proj(2 * c, 3 * c)
    h_ref[...] = cg * xc
    bg_ref[...] = proj(c, 2 * c)
    o = 3 * c
    q_ref[...] = (proj(o, o + a) * (NA_HEAD_DIM ** -0.5)).astype(BF16)
    k_ref[...] = proj(o + a, o + 2 * a)
    v_ref[...] = proj(o + 2 * a, o + 3 * a)
    o = o + 3 * a
    gc_ref[...] = jax.nn.sigmoid(proj(o, o + D_MODEL))
    gn_ref[...] = jax.nn.sigmoid(proj(o + D_MODEL, o + 2 * D_MODEL))


def _inproj(x_all, mods3, w_in_bf):
    row = lambda w: pl.BlockSpec((ROW_TILE, w), lambda i: (i, 0))
    return pl.pallas_call(
        _inproj_kernel,
        grid=(N_TILES,),
        in_specs=[row(D_MODEL),
                  pl.BlockSpec((1, 1, 6 * D_MODEL), lambda i: (_tile_mod_row(i), 0, 0)),
                  pl.BlockSpec((D_MODEL, IN_COLS), lambda i: (0, 0))],
        out_specs=[row(CONV_WIDTH), row(CONV_WIDTH), row(NA_WIDTH), row(NA_WIDTH), row(NA_WIDTH),
                   row(D_MODEL), row(D_MODEL)],
        out_shape=[jax.ShapeDtypeStruct((N_TOK, CONV_WIDTH), F32),
                   jax.ShapeDtypeStruct((N_TOK, CONV_WIDTH), F32),
                   jax.ShapeDtypeStruct((N_TOK, NA_WIDTH), BF16),
                   jax.ShapeDtypeStruct((N_TOK, NA_WIDTH), F32),
                   jax.ShapeDtypeStruct((N_TOK, NA_WIDTH), F32),
                   jax.ShapeDtypeStruct((N_TOK, D_MODEL), F32),
                   jax.ShapeDtypeStruct((N_TOK, D_MODEL), F32)],
        compiler_params=_cparams(("parallel",)),
        name="inproj",
    )(x_all, mods3, w_in_bf)


def _head_pair_attention(q2, kv_list, bias_list):
    lane = lax.broadcasted_iota(jnp.int32, q2.shape, 1)
    outs = []
    for hh in range(2):
        in_head = (lane >= hh * NA_HEAD_DIM) & (lane < (hh + 1) * NA_HEAD_DIM)
        qh = jnp.where(in_head, q2, jnp.zeros_like(q2))
        scores = []
        for (k2, _), bias in zip(kv_list, bias_list):
            s = lax.dot_general(qh, k2, (((1,), (1,)), ((), ())), preferred_element_type=F32)
            if bias is not None:
                s = s + bias(hh)
            scores.append(s)
        m = scores[0].max(axis=-1, keepdims=True)
        for s in scores[1:]:
            m = jnp.maximum(m, s.max(axis=-1, keepdims=True))
        acc = None
        den = None
        for s, (_, v2) in zip(scores, kv_list):
            p = jnp.exp(s - m)
            d = p.sum(axis=-1, keepdims=True)
            o = jnp.dot(p.astype(BF16), v2, preferred_element_type=F32)
            acc = o if acc is None else acc + o
            den = d if den is None else den + d
        outs.append(acc / den)
    return jnp.where(lane < NA_HEAD_DIM, outs[0], outs[1])


def _ctx_attn_kernel(q_ref, k_ref, v_ref, o_ref):
    k2 = k_ref[...].astype(BF16)
    v2 = v_ref[...].astype(BF16)
    o_ref[...] = _head_pair_attention(q_ref[...], [(k2, v2)], [None]).astype(BF16)


def _ctx_attn(q, k, v):
    spec = pl.BlockSpec((SEQ, 128), lambda b, p: (b, p))
    return pl.pallas_call(
        _ctx_attn_kernel,
        grid=(BATCH, NA_WIDTH // 128),
        in_specs=[spec, spec, spec],
        out_specs=spec,
        out_shape=jax.ShapeDtypeStruct((N_CTX, NA_WIDTH), BF16),
        compiler_params=_cparams(("parallel", "parallel")),
        name="ctx_attn",
    )(q, k, v)


def _lat_attn_kernel(q_ref, k_ref, v_ref, kc_ref, vc_ref, bias_ref, o_ref):
    kw = k_ref[...].astype(BF16)
    vw = v_ref[...].astype(BF16)
    kc = kc_ref[0].astype(BF16)
    vc = vc_ref[0].astype(BF16)
    o = _head_pair_attention(q_ref[...], [(kw, vw), (kc, vc)], [lambda hh: bias_ref[hh], None])
    o_ref[...] = o.astype(BF16)


LAT_QB = 256


def _lat_attn(q, k, v, k_ctx, v_ctx, bias):
    n_qb = DEC_SEQ // LAT_QB
    q_off = N_CTX // LAT_QB
    kv_off = N_CTX // DEC_SEQ
    return pl.pallas_call(
        _lat_attn_kernel,
        grid=(n_qb, NA_WIDTH // 128, DEC_BATCH),
        in_specs=[pl.BlockSpec((LAT_QB, 128), lambda qb, p, b: (q_off + b * n_qb + qb, p)),
                  pl.BlockSpec((DEC_SEQ, 128), lambda qb, p, b: (kv_off + b, p)),
                  pl.BlockSpec((DEC_SEQ, 128), lambda qb, p, b: (kv_off + b, p)),
                  pl.BlockSpec((1, PAST_LEN, 128), lambda qb, p, b: (b, 0, p)),
                  pl.BlockSpec((1, PAST_LEN, 128), lambda qb, p, b: (b, 0, p)),
                  pl.BlockSpec((2, LAT_QB, DEC_SEQ), lambda qb, p, b: (p, qb, 0))],
        out_specs=pl.BlockSpec((LAT_QB, 128), lambda qb, p, b: (b * n_qb + qb, p)),
        out_shape=jax.ShapeDtypeStruct((N_LAT, NA_WIDTH), BF16),
        compiler_params=_cparams(("parallel", "parallel", "parallel")),
        name="lat_attn",
    )(q, k, v, k_ctx, v_ctx, bias)


def _window_bias(rpb):
    kh = min(WIN_H, GRID_ROWS)
    n_dr = 2 * WIN_H - 1
    col = np.arange(GRID_W)
    col_start = np.clip(col - WIN_W // 2, 0, GRID_W - WIN_W)
    col_in = (col[None, :] >= col_start[:, None]) & (col[None, :] < col_start[:, None] + WIN_W)
    row = np.arange(GRID_ROWS)
    row_start = np.clip(row - kh // 2, 0, GRID_ROWS - kh)
    row_in = (row[None, :] >= row_start[:, None]) & (row[None, :] < row_start[:, None] + kh)

    lpad = GRID_W - WIN_W
    padded = jnp.pad(rpb.astype(F32), ((0, 0), (0, 0), (lpad, lpad)))
    cols = jnp.stack([padded[:, :, GRID_W - 1 - qc:2 * GRID_W - 1 - qc] for qc in range(GRID_W)], axis=2)
    cols = jnp.where(jnp.asarray(col_in)[None, None], cols, NEG_INF)
    cols = jnp.pad(cols, ((0, 0), (GRID_ROWS, GRID_ROWS), (0, 0), (0, 0)))
    full = jnp.stack([cols[:, GRID_ROWS + WIN_H - 1 - qr:2 * GRID_ROWS + WIN_H - 1 - qr]
                      for qr in range(GRID_ROWS)], axis=1)
    full = jnp.where(jnp.asarray(row_in)[None, :, :, None, None], full, NEG_INF)
    return full.transpose(0, 1, 3, 2, 4).reshape(NA_HEADS, DEC_SEQ, DEC_SEQ)


def _layer_norm(x, g, b):
    mu = jnp.mean(x, axis=-1, keepdims=True)
    xc = x - mu
    var = jnp.mean(xc * xc, axis=-1, keepdims=True)
    return xc * lax.rsqrt(var + LN_EPS) * g + b


def _merge_kernel(x_ref, mod_ref, h_ref, hp_ref, hn_ref, bg_ref, yna_ref, gc_ref, gn_ref,
                  cw_ref, wco_ref, wno_ref, wo_ref, g1_ref, b1_ref, wqt_ref, sk_ref,
                  x1_ref, u2_ref, st_ref):
    i = pl.program_id(0)
    pos = (i - CTX_TILES) % LAT_TILES_PER_SEQ
    is_ctx = i < CTX_TILES
    has_prev = jnp.where(is_ctx | (pos == 0), 0.0, 1.0).astype(F32)
    has_next = jnp.where(is_ctx | (pos == LAT_TILES_PER_SEQ - 1), 0.0, 1.0).astype(F32)

    h = h_ref[...]
    rows = lax.broadcasted_iota(jnp.int32, h.shape, 0)
    prev_row = hp_ref[7:8, :] * has_prev
    next_row = hn_ref[0:1, :] * has_next
    h_prev = jnp.where(rows == 0, prev_row, pltpu.roll(h, 1, 0))
    h_next = jnp.where(rows == ROW_TILE - 1, next_row, pltpu.roll(h, ROW_TILE - 1, 0))
    y = cw_ref[0:1, :] * h_prev
    y = y + cw_ref[1:2, :] * h
    y = y + cw_ref[2:3, :] * h_next
    y_conv = (bg_ref[...] * y).astype(BF16)

    merged = (gc_ref[...] * jnp.dot(y_conv, wco_ref[...], preferred_element_type=F32) +
              gn_ref[...] * jnp.dot(yna_ref[...], wno_ref[...], preferred_element_type=F32))
    z = jnp.dot(merged.astype(BF16), wo_ref[...], preferred_element_type=F32)

    ga1 = mod_ref[0, :, 2 * D_MODEL:3 * D_MODEL]
    sh2 = mod_ref[0, :, 3 * D_MODEL:4 * D_MODEL]
    sc2 = mod_ref[0, :, 4 * D_MODEL:5 * D_MODEL]
    x1 = _layer_norm(ALPHA * x_ref[...] + ga1 * z, g1_ref[...], b1_ref[...])
    x1_ref[...] = x1
    u2 = (x1 * (1 + sc2) + sh2).astype(BF16)
    u2_ref[...] = u2

    qt = lax.dot_general(wqt_ref[...], u2, (((1,), (1,)), ((), ())), preferred_element_type=F32)
    half = PEER_KEY_DIM // 2
    for hp in range(2 * PEER_HEADS):
        blk = qt[hp * half:(hp + 1) * half, :].astype(BF16)
        st_ref[hp * N_KEYS:(hp + 1) * N_KEYS, :] = jnp.dot(sk_ref[hp], blk, preferred_element_type=F32)


def _merge(x_all, mods3, h, bg, y_na, gc, gn, conv_w, wco, wno, wo, g1, b1, wqt, sk):
    row = lambda w: pl.BlockSpec((ROW_TILE, w), lambda i: (i, 0))
    full = lambda *s: pl.BlockSpec(s, lambda i: (0,) * len(s))
    eights = ROW_TILE // 8
    n8 = N_TOK // 8
    return pl.pallas_call(
        _merge_kernel,
        grid=(N_TILES,),
        in_specs=[row(D_MODEL),
                  pl.BlockSpec((1, 1, 6 * D_MODEL), lambda i: (_tile_mod_row(i), 0, 0)),
                  row(CONV_WIDTH),
                  pl.BlockSpec((8, CONV_WIDTH), lambda i: (jnp.maximum(i * eights - 1, 0), 0)),
                  pl.BlockSpec((8, CONV_WIDTH), lambda i: (jnp.minimum((i + 1) * eights, n8 - 1), 0)),
                  row(CONV_WIDTH), row(NA_WIDTH), row(D_MODEL), row(D_MODEL),
                  full(CONV_K, CONV_WIDTH), full(CONV_WIDTH, D_MODEL), full(NA_WIDTH, D_MODEL),
                  full(D_MODEL, D_MODEL), full(1, D_MODEL), full(1, D_MODEL),
                  full(PEER_HEADS * PEER_KEY_DIM, D_MODEL),
                  full(2 * PEER_HEADS, N_KEYS, PEER_KEY_DIM // 2)],
        out_specs=[row(D_MODEL), row(D_MODEL),
                   pl.BlockSpec((2 * PEER_HEADS * N_KEYS, ROW_TILE), lambda i: (0, i))],
        out_shape=[jax.ShapeDtypeStruct((N_TOK, D_MODEL), F32),
                   jax.ShapeDtypeStruct((N_TOK, D_MODEL), BF16),
                   jax.ShapeDtypeStruct((2 * PEER_HEADS * N_KEYS, N_TOK), F32)],
        compiler_params=_cparams(("parallel",)),
        name="merge",
    )(x_all, mods3, h, h, h, bg, y_na, gc, gn, conv_w, wco, wno, wo, g1, b1, wqt, sk)


def _sublane_max(x):
    x = jnp.maximum(x, pltpu.roll(x, 4, 0))
    x = jnp.maximum(x, pltpu.roll(x, 2, 0))
    return jnp.maximum(x, pltpu.roll(x, 1, 0))


def _select_kernel(st_ref, tau_ref, m0_ref, m1_ref, zinv_ref, top_ref):
    n_chunks = N_KEYS // 8

    def one_list(hp, carry):
        base = pl.multiple_of(hp * N_KEYS, N_KEYS)
        cur = [st_ref[pl.ds(base + 8 * c, 8), :] for c in range(n_chunks)]
        for r in range(N_RANK):
            m = cur[0]
            for c in range(1, n_chunks):
                m = jnp.maximum(m, cur[c])
            m = _sublane_max(m)
            top_ref[hp, r] = m
            if r + 1 < N_RANK:
                cur = [jnp.where(x == m, -jnp.inf, x) for x in cur]
        return carry

    lax.fori_loop(0, 2 * PEER_HEADS, one_list, 0)

    sub = lax.broadcasted_iota(jnp.int32, (PEER_HEADS, SEL_TOK), 0)

    def packed(p, r):
        out = top_ref[p, r]
        for hd in range(1, PEER_HEADS):
            out = jnp.where(sub == hd, top_ref[2 * hd + p, r], out)
        return out

    a = [packed(0, r) for r in range(N_RANK)]
    b = [packed(1, r) for r in range(N_RANK)]
    cand = [a[r0] + b[r1] for (r0, r1) in STAIR]

    def next_below(t):
        nxt = None
        for x in cand[1:]:
            y = jnp.where(x < t, x, -jnp.inf)
            nxt = y if nxt is None else jnp.maximum(nxt, y)
        return nxt

    top = cand[0]
    t = top
    c16 = jnp.full_like(top, -jnp.inf)
    c17 = jnp.full_like(top, -jnp.inf)
    for step in range(N_RANK):
        if step > 0:
            t = next_below(t)
        cnt = jnp.zeros_like(top)
        for x in cand:
            cnt = cnt + jnp.where(x >= t, 1.0, 0.0)
        c16 = jnp.maximum(c16, jnp.where(cnt >= PEER_TOPK, t, -jnp.inf))
        c17 = jnp.maximum(c17, jnp.where(cnt >= N_RANK, t, -jnp.inf))
    t = 0.5 * (c16 + c17)
    z = jnp.ones_like(top)
    for x in cand[1:]:
        z = z + jnp.where(x >= t, jnp.exp(x - top), 0.0)
    tau_ref[...] = t
    m0_ref[...] = a[0]
    m1_ref[...] = b[0]
    zinv_ref[...] = 1.0 / z


def _select(st):
    n_tok = st.shape[1]
    stat = pl.BlockSpec((PEER_HEADS, SEL_TOK), lambda i: (0, i))
    shp = jax.ShapeDtypeStruct((PEER_HEADS, n_tok), F32)
    return pl.pallas_call(
        _select_kernel,
        grid=(n_tok // SEL_TOK,),
        in_specs=[pl.BlockSpec((2 * PEER_HEADS * N_KEYS, SEL_TOK), lambda i: (0, i))],
        out_specs=[stat, stat, stat, stat],
        out_shape=[shp, shp, shp, shp],
        scratch_shapes=[pltpu.VMEM((2 * PEER_HEADS, N_RANK, 8, SEL_TOK), F32)],
        compiler_params=_cparams(("parallel",)),
        name="select",
    )(st)


def _gelu(x):
    return 0.5 * x * (1.0 + lax.erf(x * np.float32(np.sqrt(0.5))))


def _peer_kernel(u2_ref, st_ref, tau_ref, m0_ref, m1_ref, zinv_ref, u_ref, vt_ref,
                 x1_ref, mod_ref, g2_ref, b2_ref, o_ref,
                 thr_ref, e0_ref, e1_ref, s_ref, p_ref, acc_ref):
    eb = pl.program_id(1)

    @pl.when(eb == 0)
    def _prologue():
        for hd in range(PEER_HEADS):
            s0 = st_ref[(2 * hd) * N_KEYS:(2 * hd + 1) * N_KEYS, :]
            s1 = st_ref[(2 * hd + 1) * N_KEYS:(2 * hd + 2) * N_KEYS, :]
            rows = slice(hd * N_KEYS, (hd + 1) * N_KEYS)
            thr_ref[rows, :] = tau_ref[hd:hd + 1, :] - s0
            e0_ref[rows, :] = jnp.exp(s0 - m0_ref[hd:hd + 1, :]) * zinv_ref[hd:hd + 1, :]
            e1_ref[rows, :] = jnp.exp(s1 - m1_ref[hd:hd + 1, :])
        acc_ref[...] = jnp.zeros_like(acc_ref)

    s_ref[...] = lax.dot_general(u_ref[...], u2_ref[...], (((1,), (1,)), ((), ())),
                                 preferred_element_type=F32)

    def gate_rows(r, carry):
        row0 = pl.multiple_of(r * N_KEYS, N_KEYS)
        key0 = eb * PEER_ROWS_PER_EB + r
        for lane0 in range(0, PEER_TOK, PEER_LANE):
            lanes = slice(lane0, lane0 + PEER_LANE)
            w = jnp.zeros((N_KEYS, PEER_LANE), F32)
            for hd in range(PEER_HEADS):
                s1 = st_ref[(2 * hd + 1) * N_KEYS:(2 * hd + 2) * N_KEYS, lanes]
                e1 = e1_ref[hd * N_KEYS:(hd + 1) * N_KEYS, lanes]
                thr = thr_ref[pl.ds(hd * N_KEYS + key0, 1), lanes]
                e0 = e0_ref[pl.ds(hd * N_KEYS + key0, 1), lanes]
                w = w + jnp.where(s1 >= thr, e1, 0.0) * e0
            g = _gelu(s_ref[pl.ds(row0, N_KEYS), lanes])
            p_ref[pl.ds(row0, N_KEYS), lanes] = (w * g).astype(BF16)
        return carry

    lax.fori_loop(0, PEER_ROWS_PER_EB, gate_rows, 0)

    acc_ref[...] += jnp.dot(vt_ref[...], p_ref[...], preferred_element_type=F32)

    @pl.when(eb == pl.num_programs(1) - 1)
    def _epilogue():
        y = acc_ref[...].T
        ga2 = mod_ref[0, :, 5 * D_MODEL:6 * D_MODEL]
        o_ref[...] = _layer_norm(ALPHA * x1_ref[...] + ga2 * y, g2_ref[...], b2_ref[...])


def _peer(u2, st, tau, m0, m1, zinv, u_bf, vt_bf, x1, mods3, g2, b2):
    n_tok = u2.shape[0]
    tiles_per_seq = DEC_SEQ // PEER_TOK
    ctx_tiles = N_CTX // PEER_TOK
    mod_row = lambda t: jnp.where(t < ctx_tiles, 0, 1 + (t - ctx_tiles) // tiles_per_seq)
    stat = pl.BlockSpec((PEER_HEADS, PEER_TOK), lambda t, e: (0, t))
    return pl.pallas_call(
        _peer_kernel,
        grid=(n_tok // PEER_TOK, N_EXPERTS // PEER_EB),
        in_specs=[pl.BlockSpec((PEER_TOK, D_MODEL), lambda t, e: (t, 0)),
                  pl.BlockSpec((2 * PEER_HEADS * N_KEYS, PEER_TOK), lambda t, e: (0, t)),
                  stat, stat, stat, stat,
                  pl.BlockSpec((PEER_EB, D_MODEL), lambda t, e: (e, 0)),
                  pl.BlockSpec((D_MODEL, PEER_EB), lambda t, e: (0, e)),
                  pl.BlockSpec((PEER_TOK, D_MODEL), lambda t, e: (t, 0)),
                  pl.BlockSpec((1, 1, 6 * D_MODEL), lambda t, e: (mod_row(t), 0, 0)),
                  pl.BlockSpec((1, D_MODEL), lambda t, e: (0, 0)),
                  pl.BlockSpec((1, D_MODEL), lambda t, e: (0, 0))],
        out_specs=pl.BlockSpec((PEER_TOK, D_MODEL), lambda t, e: (t, 0)),
        out_shape=jax.ShapeDtypeStruct((n_tok, D_MODEL), F32),
        scratch_shapes=[pltpu.VMEM((PEER_HEADS * N_KEYS, PEER_TOK), F32),
                        pltpu.VMEM((PEER_HEADS * N_KEYS, PEER_TOK), F32),
                        pltpu.VMEM((PEER_HEADS * N_KEYS, PEER_TOK), F32),
                        pltpu.VMEM((PEER_EB, PEER_TOK), F32),
                        pltpu.VMEM((PEER_EB, PEER_TOK), BF16),
                        pltpu.VMEM((D_MODEL, PEER_TOK), F32)],
        compiler_params=_cparams(("parallel", "arbitrary")),
        name="peer",
    )(u2, st, tau, m0, m1, zinv, u_bf, vt_bf, x1, mods3, g2, b2)


def kernel(x_prompt, x_sample, cache_k, cache_v, c, c_ctx, w_ada, b_ada, w_in, conv_w, w_conv_out,
           w_na_out, rpb, w_o, ln1_g, ln1_b, w_query, sub_keys, expert_u, expert_v, ln2_g, ln2_b):
    l = 0
    x_all = jnp.concatenate([x_prompt.reshape(N_CTX, D_MODEL), x_sample.reshape(N_LAT, D_MODEL)], axis=0)
    cond = jnp.concatenate([c_ctx[None, :], c, jnp.zeros((N_COND - 1 - DEC_BATCH, D_MODEL), F32)], axis=0)

    mods = _mods(cond, w_ada[l], b_ada[l])
    mods3 = mods.reshape(N_COND, 1, 6 * D_MODEL)

    h, bg, q, k, v, gc, gn = _inproj(x_all, mods3, w_in[l].astype(BF16))

    y_ctx = _ctx_attn(q, k, v)
    k_ctx = cache_k[:, l].reshape(DEC_BATCH, PAST_LEN, NA_WIDTH)
    v_ctx = cache_v[:, l].reshape(DEC_BATCH, PAST_LEN, NA_WIDTH)
    y_lat = _lat_attn(q, k, v, k_ctx, v_ctx, _window_bias(rpb[l]))
    y_na = jnp.concatenate([y_ctx, y_lat], axis=0)

    wqt = w_query[l].T.astype(BF16)
    sk = sub_keys[l].reshape(2 * PEER_HEADS, N_KEYS, PEER_KEY_DIM // 2).astype(BF16)
    x1, u2, st = _merge(x_all, mods3, h, bg, y_na, gc, gn, conv_w[l],
                        w_conv_out[l].astype(BF16), w_na_out[l].astype(BF16), w_o[l].astype(BF16),
                        ln1_g[l].reshape(1, D_MODEL), ln1_b[l].reshape(1, D_MODEL), wqt, sk)

    tau, m0, m1, zinv = _select(st)
    x2 = _peer(u2, st, tau, m0, m1, zinv, expert_u[l].astype(BF16), expert_v[l].T.astype(BF16),
               x1, mods3, ln2_g[l].reshape(1, D_MODEL), ln2_b[l].reshape(1, D_MODEL))

    y_prompt = x2[:N_CTX].reshape(BATCH, SEQ, D_MODEL)
    y_sample = x2[N_CTX:].reshape(DEC_BATCH, DEC_SEQ, D_MODEL)
    new_k = k[:N_CTX].reshape(BATCH, 1, SEQ, NA_HEADS, NA_HEAD_DIM)
    new_v = v[:N_CTX].reshape(BATCH, 1, SEQ, NA_HEADS, NA_HEAD_DIM)
    return (y_prompt, y_sample, new_k, new_v)
```

```python
import functools

import numpy as np
import jax
import jax.numpy as jnp
from jax import lax
from jax.experimental import pallas as pl
from jax.experimental.pallas import tpu as pltpu

F32 = jnp.float32
BF16 = jnp.bfloat16

D_MODEL = 1024
BATCH = 16
SEQ = 256
DEC_BATCH = 8
DEC_SEQ = 1024
PAST_LEN = 512
GRID_W = 64
GRID_ROWS = DEC_SEQ // GRID_W
CONV_WIDTH = 512
CONV_K = 3
NA_HEADS = 8
NA_HEAD_DIM = 64
NA_WIDTH = NA_HEADS * NA_HEAD_DIM
WIN_H = 8
WIN_W = 16
PEER_HEADS = 8
PEER_KEY_DIM = 256
N_KEYS = 128
N_EXPERTS = N_KEYS * N_KEYS
PEER_TOPK = 16
LN_EPS = 1e-5
NEG_INF = -1e30
DEPTH = 1
ALPHA = (2 * DEPTH) ** 0.25
IN_COLS = 3 * CONV_WIDTH + 3 * NA_WIDTH + 2 * D_MODEL

N_CTX = BATCH * SEQ
N_LAT = DEC_BATCH * DEC_SEQ
N_TOK = N_CTX + N_LAT
N_COND = 16

ROW_TILE = 256
CTX_TILES = N_CTX // ROW_TILE
LAT_TILES_PER_SEQ = DEC_SEQ // ROW_TILE
N_TILES = N_TOK // ROW_TILE

PEER_TOK = 512
PEER_LANE = 256
PEER_EB = 1024
PEER_ROWS_PER_EB = PEER_EB // N_KEYS
SEL_TOK = 256

VMEM_LIMIT = 56 * 1024 * 1024

N_RANK = PEER_TOPK + 1
STAIR = [(a, b) for a in range(N_RANK) for b in range(N_RANK) if (a + 1) * (b + 1) <= N_RANK]


def _cparams(sem):
    return pltpu.CompilerParams(dimension_semantics=sem, vmem_limit_bytes=VMEM_LIMIT)


def _tile_mod_row(i):
    return jnp.where(i < CTX_TILES, 0, 1 + (i - CTX_TILES) // LAT_TILES_PER_SEQ)


def _mods_kernel(cond_ref, w_ref, b_ref, o_ref):
    cnd = cond_ref[...]
    act = cnd * jax.nn.sigmoid(cnd)
    o_ref[...] = jnp.dot(act, w_ref[...], precision=lax.Precision.HIGHEST,
                         preferred_element_type=F32) + b_ref[...]


def _mods(cond, w_ada, b_ada):
    cols = 6 * D_MODEL
    blk = 1536
    return pl.pallas_call(
        _mods_kernel,
        grid=(cols // blk,),
        in_specs=[pl.BlockSpec((N_COND, D_MODEL), lambda j: (0, 0)),
                  pl.BlockSpec((D_MODEL, blk), lambda j: (0, j)),
                  pl.BlockSpec((1, blk), lambda j: (0, j))],
        out_specs=pl.BlockSpec((N_COND, blk), lambda j: (0, j)),
        out_shape=jax.ShapeDtypeStruct((N_COND, cols), F32),
        compiler_params=_cparams(("arbitrary",)),
        name="mods",
    )(cond, w_ada, b_ada.reshape(1, cols))


def _inproj_kernel(x_ref, mod_ref, w_ref, h_ref, bg_ref, q_ref, k_ref, v_ref, gc_ref, gn_ref):
    sh1 = mod_ref[0, :, 0:D_MODEL]
    sc1 = mod_ref[0, :, D_MODEL:2 * D_MODEL]
    u = (x_ref[...] * (1 + sc1) + sh1).astype(BF16)

    def proj(lo, hi):
        return jnp.dot(u, w_ref[:, lo:hi], preferred_element_type=F32)

    c = CONV_WIDTH
    a = NA_WIDTH
    xc = proj(0, c)
    cg = proj(2 * c, 3 * c)
    h_ref[...] = cg * xc
    bg_ref[...] = proj(c, 2 * c)
    o = 3 * c
    q_ref[...] = (proj(o, o + a) * (NA_HEAD_DIM ** -0.5)).astype(BF16)
    k_ref[...] = proj(o + a, o + 2 * a)
    v_ref[...] = proj(o + 2 * a, o + 3 * a)
    o = o + 3 * a
    gc_ref[...] = jax.nn.sigmoid(proj(o, o + D_MODEL))
    gn_ref[...] = jax.nn.sigmoid(proj(o + D_MODEL, o + 2 * D_MODEL))


def _inproj(x_all, mods3, w_in_bf):
    row = lambda w: pl.BlockSpec((ROW_TILE, w), lambda i: (i, 0))
    return pl.pallas_call(
        _inproj_kernel,
        grid=(N_TILES,),
        in_specs=[row(D_MODEL),
                  pl.BlockSpec((1, 1, 6 * D_MODEL), lambda i: (_tile_mod_row(i), 0, 0)),
                  pl.BlockSpec((D_MODEL, IN_COLS), lambda i: (0, 0))],
        out_specs=[row(CONV_WIDTH), row(CONV_WIDTH), row(NA_WIDTH), row(NA_WIDTH), row(NA_WIDTH),
                   row(D_MODEL), row(D_MODEL)],
        out_shape=[jax.ShapeDtypeStruct((N_TOK, CONV_WIDTH), F32),
                   jax.ShapeDtypeStruct((N_TOK, CONV_WIDTH), F32),
                   jax.ShapeDtypeStruct((N_TOK, NA_WIDTH), BF16),
                   jax.ShapeDtypeStruct((N_TOK, NA_WIDTH), F32),
                   jax.ShapeDtypeStruct((N_TOK, NA_WIDTH), F32),
                   jax.ShapeDtypeStruct((N_TOK, D_MODEL), F32),
                   jax.ShapeDtypeStruct((N_TOK, D_MODEL), F32)],
        compiler_params=_cparams(("parallel",)),
        name="inproj",
    )(x_all, mods3, w_in_bf)


def _head_pair_attention(q2, kv_list, bias_list):
    lane = lax.broadcasted_iota(jnp.int32, q2.shape, 1)
    outs = []
    for hh in range(2):
        in_head = (lane >= hh * NA_HEAD_DIM) & (lane < (hh + 1) * NA_HEAD_DIM)
        qh = jnp.where(in_head, q2, jnp.zeros_like(q2))
        scores = []
        for (k2, _), bias in zip(kv_list, bias_list):
            s = lax.dot_general(qh, k2, (((1,), (1,)), ((), ())), preferred_element_type=F32)
            if bias is not None:
                s = s + bias(hh)
            scores.append(s)
        m = scores[0].max(axis=-1, keepdims=True)
        for s in scores[1:]:
            m = jnp.maximum(m, s.max(axis=-1, keepdims=True))
        acc = None
        den = None
        for s, (_, v2) in zip(scores, kv_list):
            p = jnp.exp(s - m)
            d = p.sum(axis=-1, keepdims=True)
            o = jnp.dot(p.astype(BF16), v2, preferred_element_type=F32)
            acc = o if acc is None else acc + o
            den = d if den is None else den + d
        outs.append(acc / den)
    return jnp.where(lane < NA_HEAD_DIM, outs[0], outs[1])


def _ctx_attn_kernel(q_ref, k_ref, v_ref, o_ref):
    k2 = k_ref[...].astype(BF16)
    v2 = v_ref[...].astype(BF16)
    o_ref[...] = _head_pair_attention(q_ref[...], [(k2, v2)], [None]).astype(BF16)


def _ctx_attn(q, k, v):
    spec = pl.BlockSpec((SEQ, 128), lambda b, p: (b, p))
    return pl.pallas_call(
        _ctx_attn_kernel,
        grid=(BATCH, NA_WIDTH // 128),
        in_specs=[spec, spec, spec],
        out_specs=spec,
        out_shape=jax.ShapeDtypeStruct((N_CTX, NA_WIDTH), BF16),
        compiler_params=_cparams(("parallel", "parallel")),
        name="ctx_attn",
    )(q, k, v)


def _lat_attn_kernel(q_ref, k_ref, v_ref, kc_ref, vc_ref, bias_ref, o_ref):
    kw = k_ref[...].astype(BF16)
    vw = v_ref[...].astype(BF16)
    kc = kc_ref[0].astype(BF16)
    vc = vc_ref[0].astype(BF16)
    o = _head_pair_attention(q_ref[...], [(kw, vw), (kc, vc)], [lambda hh: bias_ref[hh], None])
    o_ref[...] = o.astype(BF16)


LAT_QB = 256


def _lat_attn(q, k, v, k_ctx, v_ctx, bias):
    n_qb = DEC_SEQ // LAT_QB
    q_off = N_CTX // LAT_QB
    kv_off = N_CTX // DEC_SEQ
    return pl.pallas_call(
        _lat_attn_kernel,
        grid=(n_qb, NA_WIDTH // 128, DEC_BATCH),
        in_specs=[pl.BlockSpec((LAT_QB, 128), lambda qb, p, b: (q_off + b * n_qb + qb, p)),
                  pl.BlockSpec((DEC_SEQ, 128), lambda qb, p, b: (kv_off + b, p)),
                  pl.BlockSpec((DEC_SEQ, 128), lambda qb, p, b: (kv_off + b, p)),
                  pl.BlockSpec((1, PAST_LEN, 128), lambda qb, p, b: (b, 0, p)),
                  pl.BlockSpec((1, PAST_LEN, 128), lambda qb, p, b: (b, 0, p)),
                  pl.BlockSpec((2, LAT_QB, DEC_SEQ), lambda qb, p, b: (p, qb, 0))],
        out_specs=pl.BlockSpec((LAT_QB, 128), lambda qb, p, b: (b * n_qb + qb, p)),
        out_shape=jax.ShapeDtypeStruct((N_LAT, NA_WIDTH), BF16),
        compiler_params=_cparams(("parallel", "parallel", "parallel")),
        name="lat_attn",
    )(q, k, v, k_ctx, v_ctx, bias)


def _window_bias(rpb):
    kh = min(WIN_H, GRID_ROWS)
    n_dr = 2 * WIN_H - 1
    col = np.arange(GRID_W)
    col_start = np.clip(col - WIN_W // 2, 0, GRID_W - WIN_W)
    col_in = (col[None, :] >= col_start[:, None]) & (col[None, :] < col_start[:, None] + WIN_W)
    row = np.arange(GRID_ROWS)
    row_start = np.clip(row - kh // 2, 0, GRID_ROWS - kh)
    row_in = (row[None, :] >= row_start[:, None]) & (row[None, :] < row_start[:, None] + kh)

    lpad = GRID_W - WIN_W
    padded = jnp.pad(rpb.astype(F32), ((0, 0), (0, 0), (lpad, lpad)))
    cols = jnp.stack([padded[:, :, GRID_W - 1 - qc:2 * GRID_W - 1 - qc] for qc in range(GRID_W)], axis=2)
    cols = jnp.where(jnp.asarray(col_in)[None, None], cols, NEG_INF)
    cols = jnp.pad(cols, ((0, 0), (GRID_ROWS, GRID_ROWS), (0, 0), (0, 0)))
    full = jnp.stack([cols[:, GRID_ROWS + WIN_H - 1 - qr:2 * GRID_ROWS + WIN_H - 1 - qr]
                      for qr in range(GRID_ROWS)], axis=1)
    full = jnp.where(jnp.asarray(row_in)[None, :, :, None, None], full, NEG_INF)
    return full.transpose(0, 1, 3, 2, 4).reshape(NA_HEADS, DEC_SEQ, DEC_SEQ)


def _layer_norm(x, g, b):
    mu = jnp.mean(x, axis=-1, keepdims=True)
    xc = x - mu
    var = jnp.mean(xc * xc, axis=-1, keepdims=True)
    return xc * lax.rsqrt(var + LN_EPS) * g + b


def _merge_kernel(x_ref, mod_ref, h_ref, hp_ref, hn_ref, bg_ref, yna_ref, gc_ref, gn_ref,
                  cw_ref, wco_ref, wno_ref, wo_ref, g1_ref, b1_ref, wqt_ref, sk_ref,
                  x1_ref, u2_ref, st_ref):
    i = pl.program_id(0)
    pos = (i - CTX_TILES) % LAT_TILES_PER_SEQ
    is_ctx = i < CTX_TILES
    has_prev = jnp.where(is_ctx | (pos == 0), 0.0, 1.0).astype(F32)
    has_next = jnp.where(is_ctx | (pos == LAT_TILES_PER_SEQ - 1), 0.0, 1.0).astype(F32)

    h = h_ref[...]
    rows = lax.broadcasted_iota(jnp.int32, h.shape, 0)
    prev_row = hp_ref[7:8, :] * has_prev
    next_row = hn_ref[0:1, :] * has_next
    h_prev = jnp.where(rows == 0, prev_row, pltpu.roll(h, 1, 0))
    h_next = jnp.where(rows == ROW_TILE - 1, next_row, pltpu.roll(h, ROW_TILE - 1, 0))
    y = cw_ref[0:1, :] * h_prev
    y = y + cw_ref[1:2, :] * h
    y = y + cw_ref[2:3, :] * h_next
    y_conv = (bg_ref[...] * y).astype(BF16)

    merged = (gc_ref[...] * jnp.dot(y_conv, wco_ref[...], preferred_element_type=F32) +
              gn_ref[...] * jnp.dot(yna_ref[...], wno_ref[...], preferred_element_type=F32))
    z = jnp.dot(merged.astype(BF16), wo_ref[...], preferred_element_type=F32)

    ga1 = mod_ref[0, :, 2 * D_MODEL:3 * D_MODEL]
    sh2 = mod_ref[0, :, 3 * D_MODEL:4 * D_MODEL]
    sc2 = mod_ref[0, :, 4 * D_MODEL:5 * D_MODEL]
    x1 = _layer_norm(ALPHA * x_ref[...] + ga1 * z, g1_ref[...], b1_ref[...])
    x1_ref[...] = x1
    u2 = (x1 * (1 + sc2) + sh2).astype(BF16)
    u2_ref[...] = u2

    qt = lax.dot_general(wqt_ref[...], u2, (((1,), (1,)), ((), ())), preferred_element_type=F32)
    half = PEER_KEY_DIM // 2
    for hp in range(2 * PEER_HEADS):
        blk = qt[hp * half:(hp + 1) * half, :].astype(BF16)
        st_ref[hp * N_KEYS:(hp + 1) * N_KEYS, :] = jnp.dot(sk_ref[hp], blk, preferred_element_type=F32)


def _merge(x_all, mods3, h, bg, y_na, gc, gn, conv_w, wco, wno, wo, g1, b1, wqt, sk):
    row = lambda w: pl.BlockSpec((ROW_TILE, w), lambda i: (i, 0))
    full = lambda *s: pl.BlockSpec(s, lambda i: (0,) * len(s))
    eights = ROW_TILE // 8
    n8 = N_TOK // 8
    return pl.pallas_call(
        _merge_kernel,
        grid=(N_TILES,),
        in_specs=[row(D_MODEL),
                  pl.BlockSpec((1, 1, 6 * D_MODEL), lambda i: (_tile_mod_row(i), 0, 0)),
                  row(CONV_WIDTH),
                  pl.BlockSpec((8, CONV_WIDTH), lambda i: (jnp.maximum(i * eights - 1, 0), 0)),
                  pl.BlockSpec((8, CONV_WIDTH), lambda i: (jnp.minimum((i + 1) * eights, n8 - 1), 0)),
                  row(CONV_WIDTH), row(NA_WIDTH), row(D_MODEL), row(D_MODEL),
                  full(CONV_K, CONV_WIDTH), full(CONV_WIDTH, D_MODEL), full(NA_WIDTH, D_MODEL),
                  full(D_MODEL, D_MODEL), full(1, D_MODEL), full(1, D_MODEL),
                  full(PEER_HEADS * PEER_KEY_DIM, D_MODEL),
                  full(2 * PEER_HEADS, N_KEYS, PEER_KEY_DIM // 2)],
        out_specs=[row(D_MODEL), row(D_MODEL),
                   pl.BlockSpec((2 * PEER_HEADS * N_KEYS, ROW_TILE), lambda i: (0, i))],
        out_shape=[jax.ShapeDtypeStruct((N_TOK, D_MODEL), F32),
                   jax.ShapeDtypeStruct((N_TOK, D_MODEL), BF16),
                   jax.ShapeDtypeStruct((2 * PEER_HEADS * N_KEYS, N_TOK), F32)],
        compiler_params=_cparams(("parallel",)),
        name="merge",
    )(x_all, mods3, h, h, h, bg, y_na, gc, gn, conv_w, wco, wno, wo, g1, b1, wqt, sk)


def _sublane_max(x):
    x = jnp.maximum(x, pltpu.roll(x, 4, 0))
    x = jnp.maximum(x, pltpu.roll(x, 2, 0))
    return jnp.maximum(x, pltpu.roll(x, 1, 0))


def _select_kernel(st_ref, tau_ref, a_ref, m1_ref, zinv_ref, top_ref):
    n_chunks = N_KEYS // 8

    def one_list(hp, carry):
        base = pl.multiple_of(hp * N_KEYS, N_KEYS)
        cur = [st_ref[pl.ds(base + 8 * c, 8), :] for c in range(n_chunks)]
        for r in range(N_RANK):
            m = cur[0]
            for c in range(1, n_chunks):
                m = jnp.maximum(m, cur[c])
            m = _sublane_max(m)
            top_ref[hp, r] = m
            if r + 1 < N_RANK:
                cur = [jnp.where(x == m, -jnp.inf, x) for x in cur]
        return carry

    lax.fori_loop(0, 2 * PEER_HEADS, one_list, 0)

    sub = lax.broadcasted_iota(jnp.int32, (PEER_HEADS, SEL_TOK), 0)

    def packed(p, r):
        out = top_ref[p, r]
        for hd in range(1, PEER_HEADS):
            out = jnp.where(sub == hd, top_ref[2 * hd + p, r], out)
        return out

    a = [packed(0, r) for r in range(N_RANK)]
    b = [packed(1, r) for r in range(N_RANK)]
    cand = [a[r0] + b[r1] for (r0, r1) in STAIR]

    def next_below(t):
        nxt = None
        for x in cand[1:]:
            y = jnp.where(x < t, x, -jnp.inf)
            nxt = y if nxt is None else jnp.maximum(nxt, y)
        return nxt

    top = cand[0]
    t = top
    c16 = jnp.full_like(top, -jnp.inf)
    c17 = jnp.full_like(top, -jnp.inf)
    for step in range(N_RANK):
        if step > 0:
            t = next_below(t)
        cnt = jnp.zeros_like(top)
        for x in cand:
            cnt = cnt + jnp.where(x >= t, 1.0, 0.0)
        c16 = jnp.maximum(c16, jnp.where(cnt >= PEER_TOPK, t, -jnp.inf))
        c17 = jnp.maximum(c17, jnp.where(cnt >= N_RANK, t, -jnp.inf))
    t = 0.5 * (c16 + c17)
    z = jnp.ones_like(top)
    for x in cand[1:]:
        z = z + jnp.where(x >= t, jnp.exp(x - top), 0.0)
    tau_ref[...] = t
    for r in range(N_RANK):
        a_ref[r] = a[r]
    m1_ref[...] = b[0]
    zinv_ref[...] = 1.0 / z


def _select(st):
    n_tok = st.shape[1]
    stat = pl.BlockSpec((PEER_HEADS, SEL_TOK), lambda i: (0, i))
    shp = jax.ShapeDtypeStruct((PEER_HEADS, n_tok), F32)
    return pl.pallas_call(
        _select_kernel,
        grid=(n_tok // SEL_TOK,),
        in_specs=[pl.BlockSpec((2 * PEER_HEADS * N_KEYS, SEL_TOK), lambda i: (0, i))],
        out_specs=[stat, pl.BlockSpec((N_RANK, PEER_HEADS, SEL_TOK), lambda i: (0, 0, i)), stat, stat],
        out_shape=[shp, jax.ShapeDtypeStruct((N_RANK, PEER_HEADS, n_tok), F32), shp, shp],
        scratch_shapes=[pltpu.VMEM((2 * PEER_HEADS, N_RANK, 8, SEL_TOK), F32)],
        compiler_params=_cparams(("parallel",)),
        name="select",
    )(st)


def _gelu(x):
    return 0.5 * x * (1.0 + lax.erf(x * np.float32(np.sqrt(0.5))))


def _peer_kernel(u2_ref, st_ref, tau_ref, a_ref, m1_ref, zinv_ref, u_ref, vt_ref,
                 x1_ref, mod_ref, g2_ref, b2_ref, o_ref,
                 rank0_ref, e0_ref, adm1_ref, e1_ref, s_ref, p_ref, acc_ref):
    eb = pl.program_id(1)

    @pl.when(eb == 0)
    def _prologue():
        half = N_KEYS // 2
        for hd in range(PEER_HEADS):
            for lane0 in range(0, PEER_TOK, 128):
                lanes = slice(lane0, lane0 + 128)
                a = [a_ref[r, hd:hd + 1, lanes] for r in range(N_RANK)]
                tau = tau_ref[hd:hd + 1, lanes]
                for k0 in (0, half):
                    s0 = st_ref[(2 * hd) * N_KEYS + k0:(2 * hd) * N_KEYS + k0 + half, lanes]
                    s1 = st_ref[(2 * hd + 1) * N_KEYS + k0:(2 * hd + 1) * N_KEYS + k0 + half, lanes]
                    need = tau - s1
                    rank0 = jnp.zeros_like(s0)
                    adm1 = jnp.zeros_like(s1)
                    for r in range(N_RANK):
                        rank0 = jnp.where(a[r] > s0, float(r + 1), rank0)
                        adm1 = jnp.where(a[r] >= need, float(r + 1), adm1)
                    rows = slice(hd * N_KEYS + k0, hd * N_KEYS + k0 + half)
                    rank0_ref[rows, lanes] = rank0
                    adm1_ref[rows, lanes] = adm1.astype(BF16)
                    e0_ref[rows, lanes] = jnp.exp(s0 - a[0]) * zinv_ref[hd:hd + 1, lanes]
                    e1_ref[rows, lanes] = jnp.exp(s1 - m1_ref[hd:hd + 1, lanes]).astype(BF16)
        acc_ref[...] = jnp.zeros_like(acc_ref)

    s_ref[...] = lax.dot_general(u_ref[...], u2_ref[...], (((1,), (1,)), ((), ())),
                                 preferred_element_type=F32)

    def gate_rows(r, carry):
        rows = pl.ds(pl.multiple_of(r * N_KEYS, N_KEYS), N_KEYS)
        key0 = eb * PEER_ROWS_PER_EB + r
        for lane0 in range(0, PEER_TOK, PEER_LANE):
            lanes = slice(lane0, lane0 + PEER_LANE)
            w = jnp.zeros((N_KEYS, PEER_LANE), BF16)
            for hd in range(PEER_HEADS):
                adm1 = adm1_ref[hd * N_KEYS:(hd + 1) * N_KEYS, lanes]
                e1 = e1_ref[hd * N_KEYS:(hd + 1) * N_KEYS, lanes]
                rank0 = rank0_ref[pl.ds(hd * N_KEYS + key0, 1), lanes].astype(BF16)
                e0 = e0_ref[pl.ds(hd * N_KEYS + key0, 1), lanes].astype(BF16)
                w = w + jnp.where(rank0 < adm1, e1, jnp.zeros_like(e1)) * e0
            p_ref[rows, lanes] = w * _gelu(s_ref[rows, lanes]).astype(BF16)
        return carry

    lax.fori_loop(0, PEER_ROWS_PER_EB, gate_rows, 0)

    acc_ref[...] += jnp.dot(vt_ref[...], p_ref[...], preferred_element_type=F32)

    @pl.when(eb == pl.num_programs(1) - 1)
    def _epilogue():
        y = acc_ref[...].T
        ga2 = mod_ref[0, :, 5 * D_MODEL:6 * D_MODEL]
        o_ref[...] = _layer_norm(ALPHA * x1_ref[...] + ga2 * y, g2_ref[...], b2_ref[...])


def _peer(u2, st, tau, a_sorted, m1, zinv, u_bf, vt_bf, x1, mods3, g2, b2):
    n_tok = u2.shape[0]
    tiles_per_seq = DEC_SEQ // PEER_TOK
    ctx_tiles = N_CTX // PEER_TOK
    mod_row = lambda t: jnp.where(t < ctx_tiles, 0, 1 + (t - ctx_tiles) // tiles_per_seq)
    stat = pl.BlockSpec((PEER_HEADS, PEER_TOK), lambda t, e: (0, t))
    return pl.pallas_call(
        _peer_kernel,
        grid=(n_tok // PEER_TOK, N_EXPERTS // PEER_EB),
        in_specs=[pl.BlockSpec((PEER_TOK, D_MODEL), lambda t, e: (t, 0)),
                  pl.BlockSpec((2 * PEER_HEADS * N_KEYS, PEER_TOK), lambda t, e: (0, t)),
                  stat,
                  pl.BlockSpec((N_RANK, PEER_HEADS, PEER_TOK), lambda t, e: (0, 0, t)),
                  stat, stat,
                  pl.BlockSpec((PEER_EB, D_MODEL), lambda t, e: (e, 0)),
                  pl.BlockSpec((D_MODEL, PEER_EB), lambda t, e: (0, e)),
                  pl.BlockSpec((PEER_TOK, D_MODEL), lambda t, e: (t, 0)),
                  pl.BlockSpec((1, 1, 6 * D_MODEL), lambda t, e: (mod_row(t), 0, 0)),
                  pl.BlockSpec((1, D_MODEL), lambda t, e: (0, 0)),
                  pl.BlockSpec((1, D_MODEL), lambda t, e: (0, 0))],
        out_specs=pl.BlockSpec((PEER_TOK, D_MODEL), lambda t, e: (t, 0)),
        out_shape=jax.ShapeDtypeStruct((n_tok, D_MODEL), F32),
        scratch_shapes=[pltpu.VMEM((PEER_HEADS * N_KEYS, PEER_TOK), F32),
                        pltpu.VMEM((PEER_HEADS * N_KEYS, PEER_TOK), F32),
                        pltpu.VMEM((PEER_HEADS * N_KEYS, PEER_TOK), BF16),
                        pltpu.VMEM((PEER_HEADS * N_KEYS, PEER_TOK), BF16),
                        pltpu.VMEM((PEER_EB, PEER_TOK), F32),
                        pltpu.VMEM((PEER_EB, PEER_TOK), BF16),
                        pltpu.VMEM((D_MODEL, PEER_TOK), F32)],
        compiler_params=_cparams(("parallel", "arbitrary")),
        name="peer",
    )(u2, st, tau, a_sorted, m1, zinv, u_bf, vt_bf, x1, mods3, g2, b2)


def kernel(x_prompt, x_sample, cache_k, cache_v, c, c_ctx, w_ada, b_ada, w_in, conv_w, w_conv_out,
           w_na_out, rpb, w_o, ln1_g, ln1_b, w_query, sub_keys, expert_u, expert_v, ln2_g, ln2_b):
    l = 0
    x_all = jnp.concatenate([x_prompt.reshape(N_CTX, D_MODEL), x_sample.reshape(N_LAT, D_MODEL)], axis=0)
    cond = jnp.concatenate([c_ctx[None, :], c, jnp.zeros((N_COND - 1 - DEC_BATCH, D_MODEL), F32)], axis=0)

    mods = _mods(cond, w_ada[l], b_ada[l])
    mods3 = mods.reshape(N_COND, 1, 6 * D_MODEL)

    h, bg, q, k, v, gc, gn = _inproj(x_all, mods3, w_in[l].astype(BF16))

    y_ctx = _ctx_attn(q, k, v)
    k_ctx = cache_k[:, l].reshape(DEC_BATCH, PAST_LEN, NA_WIDTH)
    v_ctx = cache_v[:, l].reshape(DEC_BATCH, PAST_LEN, NA_WIDTH)
    y_lat = _lat_attn(q, k, v, k_ctx, v_ctx, _window_bias(rpb[l]))
    y_na = jnp.concatenate([y_ctx, y_lat], axis=0)

    wqt = w_query[l].T.astype(BF16)
    sk = sub_keys[l].reshape(2 * PEER_HEADS, N_KEYS, PEER_KEY_DIM // 2).astype(BF16)
    x1, u2, st = _merge(x_all, mods3, h, bg, y_na, gc, gn, conv_w[l],
                        w_conv_out[l].astype(BF16), w_na_out[l].astype(BF16), w_o[l].astype(BF16),
                        ln1_g[l].reshape(1, D_MODEL), ln1_b[l].reshape(1, D_MODEL), wqt, sk)

    tau, a_sorted, m1, zinv = _select(st)
    x2 = _peer(u2, st, tau, a_sorted, m1, zinv, expert_u[l].astype(BF16), expert_v[l].T.astype(BF16),
               x1, mods3, ln2_g[l].reshape(1, D_MODEL), ln2_b[l].reshape(1, D_MODEL))

    y_prompt = x2[:N_CTX].reshape(BATCH, SEQ, D_MODEL)
    y_sample = x2[N_CTX:].reshape(DEC_BATCH, DEC_SEQ, D_MODEL)
    new_k = k[:N_CTX].reshape(BATCH, 1, SEQ, NA_HEADS, NA_HEAD_DIM)
    new_v = v[:N_CTX].reshape(BATCH, 1, SEQ, NA_HEADS, NA_HEAD_DIM)
    return (y_prompt, y_sample, new_k, new_v)
```

```python
import functools

import numpy as np
import jax
import jax.numpy as jnp
from jax import lax
from jax.experimental import pallas as pl
from jax.experimental.pallas import tpu as pltpu

F32 = jnp.float32
BF16 = jnp.bfloat16

D_MODEL = 1024
BATCH = 16
SEQ = 256
DEC_BATCH = 8
DEC_SEQ = 1024
PAST_LEN = 512
GRID_W = 64
GRID_ROWS = DEC_SEQ // GRID_W
CONV_WIDTH = 512
CONV_K = 3
NA_HEADS = 8
NA_HEAD_DIM = 64
NA_WIDTH = NA_HEADS * NA_HEAD_DIM
WIN_H = 8
WIN_W = 16
PEER_HEADS = 8
PEER_KEY_DIM = 256
N_KEYS = 128
N_EXPERTS = N_KEYS * N_KEYS
PEER_TOPK = 16
LN_EPS = 1e-5
NEG_INF = -1e30
DEPTH = 1
ALPHA = (2 * DEPTH) ** 0.25
IN_COLS = 3 * CONV_WIDTH + 3 * NA_WIDTH + 2 * D_MODEL

N_CTX = BATCH * SEQ
N_LAT = DEC_BATCH * DEC_SEQ
N_TOK = N_CTX + N_LAT
N_COND = 16

ROW_TILE = 256
CTX_TILES = N_CTX // ROW_TILE
LAT_TILES_PER_SEQ = DEC_SEQ // ROW_TILE
N_TILES = N_TOK // ROW_TILE

PEER_TOK = 512
PEER_LANE = 256
PEER_EB = 1024
PEER_ROWS_PER_EB = PEER_EB // N_KEYS
SEL_TOK = 256

VMEM_LIMIT = 56 * 1024 * 1024

N_RANK = PEER_TOPK + 1
STAIR = [(a, b) for a in range(N_RANK) for b in range(N_RANK) if (a + 1) * (b + 1) <= N_RANK]


def _cparams(sem):
    return pltpu.CompilerParams(dimension_semantics=sem, vmem_limit_bytes=VMEM_LIMIT)


def _tile_mod_row(i):
    return jnp.where(i < CTX_TILES, 0, 1 + (i - CTX_TILES) // LAT_TILES_PER_SEQ)


def _mods_kernel(cond_ref, w_ref, b_ref, o_ref):
    cnd = cond_ref[...]
    act = cnd * jax.nn.sigmoid(cnd)
    o_ref[...] = jnp.dot(act, w_ref[...], precision=lax.Precision.HIGHEST,
                         preferred_element_type=F32) + b_ref[...]


def _mods(cond, w_ada, b_ada):
    cols = 6 * D_MODEL
    blk = 1536
    return pl.pallas_call(
        _mods_kernel,
        grid=(cols // blk,),
        in_specs=[pl.BlockSpec((N_COND, D_MODEL), lambda j: (0, 0)),
                  pl.BlockSpec((D_MODEL, blk), lambda j: (0, j)),
                  pl.BlockSpec((1, blk), lambda j: (0, j))],
        out_specs=pl.BlockSpec((N_COND, blk), lambda j: (0, j)),
        out_shape=jax.ShapeDtypeStruct((N_COND, cols), F32),
        compiler_params=_cparams(("arbitrary",)),
        name="mods",
    )(cond, w_ada, b_ada.reshape(1, cols))


def _ctx_rows(width):
    return pl.BlockSpec((ROW_TILE, width), lambda i: (jnp.minimum(i, CTX_TILES - 1), 0))


def _lat_rows(width):
    return pl.BlockSpec((ROW_TILE, width), lambda i: (jnp.maximum(i - CTX_TILES, 0), 0))


def _tile_of(ctx_ref, lat_ref):
    return jnp.where(pl.program_id(0) < CTX_TILES, ctx_ref[...], lat_ref[...])


def _inproj_kernel(xc_ref, xl_ref, mod_ref, w_ref, h_ref, bg_ref, q_ref, k_ref, v_ref, gc_ref, gn_ref):
    sh1 = mod_ref[0, :, 0:D_MODEL]
    sc1 = mod_ref[0, :, D_MODEL:2 * D_MODEL]
    u = (_tile_of(xc_ref, xl_ref) * (1 + sc1) + sh1).astype(BF16)

    def proj(lo, hi):
        return jnp.dot(u, w_ref[:, lo:hi], preferred_element_type=F32)

    c = CONV_WIDTH
    a = NA_WIDTH
    xc = proj(0, c)
    cg = proj(2 * c, 3 * c)
    h_ref[...] = cg * xc
    bg_ref[...] = proj(c, 2 * c)
    o = 3 * c
    q_ref[...] = (proj(o, o + a) * (NA_HEAD_DIM ** -0.5)).astype(BF16)
    k_ref[...] = proj(o + a, o + 2 * a)
    v_ref[...] = proj(o + 2 * a, o + 3 * a)
    o = o + 3 * a
    gc_ref[...] = jax.nn.sigmoid(proj(o, o + D_MODEL))
    gn_ref[...] = jax.nn.sigmoid(proj(o + D_MODEL, o + 2 * D_MODEL))


def _inproj(x_ctx, x_lat, mods3, w_in_bf):
    row = lambda w: pl.BlockSpec((ROW_TILE, w), lambda i: (i, 0))
    return pl.pallas_call(
        _inproj_kernel,
        grid=(N_TILES,),
        in_specs=[_ctx_rows(D_MODEL), _lat_rows(D_MODEL),
                  pl.BlockSpec((1, 1, 6 * D_MODEL), lambda i: (_tile_mod_row(i), 0, 0)),
                  pl.BlockSpec((D_MODEL, IN_COLS), lambda i: (0, 0))],
        out_specs=[row(CONV_WIDTH), row(CONV_WIDTH), row(NA_WIDTH), row(NA_WIDTH), row(NA_WIDTH),
                   row(D_MODEL), row(D_MODEL)],
        out_shape=[jax.ShapeDtypeStruct((N_TOK, CONV_WIDTH), F32),
                   jax.ShapeDtypeStruct((N_TOK, CONV_WIDTH), F32),
                   jax.ShapeDtypeStruct((N_TOK, NA_WIDTH), BF16),
                   jax.ShapeDtypeStruct((N_TOK, NA_WIDTH), F32),
                   jax.ShapeDtypeStruct((N_TOK, NA_WIDTH), F32),
                   jax.ShapeDtypeStruct((N_TOK, D_MODEL), F32),
                   jax.ShapeDtypeStruct((N_TOK, D_MODEL), F32)],
        compiler_params=_cparams(("parallel",)),
        name="inproj",
    )(x_ctx, x_lat, mods3, w_in_bf)


def _head_pair_attention(q2, kv_list, bias_list):
    lane = lax.broadcasted_iota(jnp.int32, q2.shape, 1)
    outs = []
    for hh in range(2):
        in_head = (lane >= hh * NA_HEAD_DIM) & (lane < (hh + 1) * NA_HEAD_DIM)
        qh = jnp.where(in_head, q2, jnp.zeros_like(q2))
        scores = []
        for (k2, _), bias in zip(kv_list, bias_list):
            s = lax.dot_general(qh, k2, (((1,), (1,)), ((), ())), preferred_element_type=F32)
            if bias is not None:
                s = s + bias(hh)
            scores.append(s)
        m = scores[0].max(axis=-1, keepdims=True)
        for s in scores[1:]:
            m = jnp.maximum(m, s.max(axis=-1, keepdims=True))
        acc = None
        den = None
        for s, (_, v2) in zip(scores, kv_list):
            p = jnp.exp(s - m)
            d = p.sum(axis=-1, keepdims=True)
            o = jnp.dot(p.astype(BF16), v2, preferred_element_type=F32)
            acc = o if acc is None else acc + o
            den = d if den is None else den + d
        outs.append(acc / den)
    return jnp.where(lane < NA_HEAD_DIM, outs[0], outs[1])


def _ctx_attn_kernel(q_ref, k_ref, v_ref, o_ref):
    k2 = k_ref[...].astype(BF16)
    v2 = v_ref[...].astype(BF16)
    o_ref[...] = _head_pair_attention(q_ref[...], [(k2, v2)], [None]).astype(BF16)


def _ctx_attn(q, k, v):
    spec = pl.BlockSpec((SEQ, 128), lambda b, p: (b, p))
    return pl.pallas_call(
        _ctx_attn_kernel,
        grid=(BATCH, NA_WIDTH // 128),
        in_specs=[spec, spec, spec],
        out_specs=spec,
        out_shape=jax.ShapeDtypeStruct((N_CTX, NA_WIDTH), BF16),
        compiler_params=_cparams(("parallel", "parallel")),
        name="ctx_attn",
    )(q, k, v)


def _lat_attn_kernel(q_ref, k_ref, v_ref, kc_ref, vc_ref, bias_ref, o_ref):
    kw = k_ref[...].astype(BF16)
    vw = v_ref[...].astype(BF16)
    kc = kc_ref[0].astype(BF16)
    vc = vc_ref[0].astype(BF16)
    o = _head_pair_attention(q_ref[...], [(kw, vw), (kc, vc)], [lambda hh: bias_ref[hh], None])
    o_ref[...] = o.astype(BF16)


LAT_QB = 256


def _lat_attn(q, k, v, k_ctx, v_ctx, bias):
    n_qb = DEC_SEQ // LAT_QB
    q_off = N_CTX // LAT_QB
    kv_off = N_CTX // DEC_SEQ
    return pl.pallas_call(
        _lat_attn_kernel,
        grid=(n_qb, NA_WIDTH // 128, DEC_BATCH),
        in_specs=[pl.BlockSpec((LAT_QB, 128), lambda qb, p, b: (q_off + b * n_qb + qb, p)),
                  pl.BlockSpec((DEC_SEQ, 128), lambda qb, p, b: (kv_off + b, p)),
                  pl.BlockSpec((DEC_SEQ, 128), lambda qb, p, b: (kv_off + b, p)),
                  pl.BlockSpec((1, PAST_LEN, 128), lambda qb, p, b: (b, 0, p)),
                  pl.BlockSpec((1, PAST_LEN, 128), lambda qb, p, b: (b, 0, p)),
                  pl.BlockSpec((2, LAT_QB, DEC_SEQ), lambda qb, p, b: (p, qb, 0))],
        out_specs=pl.BlockSpec((LAT_QB, 128), lambda qb, p, b: (b * n_qb + qb, p)),
        out_shape=jax.ShapeDtypeStruct((N_LAT, NA_WIDTH), BF16),
        compiler_params=_cparams(("parallel", "parallel", "parallel")),
        name="lat_attn",
    )(q, k, v, k_ctx, v_ctx, bias)


def _window_bias(rpb):
    kh = min(WIN_H, GRID_ROWS)
    n_dr = 2 * WIN_H - 1
    col = np.arange(GRID_W)
    col_start = np.clip(col - WIN_W // 2, 0, GRID_W - WIN_W)
    col_in = (col[None, :] >= col_start[:, None]) & (col[None, :] < col_start[:, None] + WIN_W)
    row = np.arange(GRID_ROWS)
    row_start = np.clip(row - kh // 2, 0, GRID_ROWS - kh)
    row_in = (row[None, :] >= row_start[:, None]) & (row[None, :] < row_start[:, None] + kh)

    lpad = GRID_W - WIN_W
    padded = jnp.pad(rpb.astype(F32), ((0, 0), (0, 0), (lpad, lpad)))
    cols = jnp.stack([padded[:, :, GRID_W - 1 - qc:2 * GRID_W - 1 - qc] for qc in range(GRID_W)], axis=2)
    cols = jnp.where(jnp.asarray(col_in)[None, None], cols, NEG_INF)
    cols = jnp.pad(cols, ((0, 0), (GRID_ROWS, GRID_ROWS), (0, 0), (0, 0)))
    full = jnp.stack([cols[:, GRID_ROWS + WIN_H - 1 - qr:2 * GRID_ROWS + WIN_H - 1 - qr]
                      for qr in range(GRID_ROWS)], axis=1)
    full = jnp.where(jnp.asarray(row_in)[None, :, :, None, None], full, NEG_INF)
    return full.transpose(0, 1, 3, 2, 4).reshape(NA_HEADS, DEC_SEQ, DEC_SEQ)


def _layer_norm(x, g, b):
    mu = jnp.mean(x, axis=-1, keepdims=True)
    xc = x - mu
    var = jnp.mean(xc * xc, axis=-1, keepdims=True)
    return xc * lax.rsqrt(var + LN_EPS) * g + b


def _merge_kernel(xc_ref, xl_ref, mod_ref, h_ref, hp_ref, hn_ref, bg_ref, ync_ref, ynl_ref, gc_ref, gn_ref,
                  cw_ref, wco_ref, wno_ref, wo_ref, g1_ref, b1_ref, wqt_ref, sk_ref,
                  x1_ref, u2_ref, st_ref):
    i = pl.program_id(0)
    pos = (i - CTX_TILES) % LAT_TILES_PER_SEQ
    is_ctx = i < CTX_TILES
    has_prev = jnp.where(is_ctx | (pos == 0), 0.0, 1.0).astype(F32)
    has_next = jnp.where(is_ctx | (pos == LAT_TILES_PER_SEQ - 1), 0.0, 1.0).astype(F32)

    h = h_ref[...]
    rows = lax.broadcasted_iota(jnp.int32, h.shape, 0)
    prev_row = hp_ref[7:8, :] * has_prev
    next_row = hn_ref[0:1, :] * has_next
    h_prev = jnp.where(rows == 0, prev_row, pltpu.roll(h, 1, 0))
    h_next = jnp.where(rows == ROW_TILE - 1, next_row, pltpu.roll(h, ROW_TILE - 1, 0))
    y = cw_ref[0:1, :] * h_prev
    y = y + cw_ref[1:2, :] * h
    y = y + cw_ref[2:3, :] * h_next
    y_conv = (bg_ref[...] * y).astype(BF16)

    merged = (gc_ref[...] * jnp.dot(y_conv, wco_ref[...], preferred_element_type=F32) +
              gn_ref[...] * jnp.dot(_tile_of(ync_ref, ynl_ref), wno_ref[...],
                                    preferred_element_type=F32))
    z = jnp.dot(merged.astype(BF16), wo_ref[...], preferred_element_type=F32)

    ga1 = mod_ref[0, :, 2 * D_MODEL:3 * D_MODEL]
    sh2 = mod_ref[0, :, 3 * D_MODEL:4 * D_MODEL]
    sc2 = mod_ref[0, :, 4 * D_MODEL:5 * D_MODEL]
    x1 = _layer_norm(ALPHA * _tile_of(xc_ref, xl_ref) + ga1 * z, g1_ref[...], b1_ref[...])
    x1_ref[...] = x1
    u2 = (x1 * (1 + sc2) + sh2).astype(BF16)
    u2_ref[...] = u2

    qt = lax.dot_general(wqt_ref[...], u2, (((1,), (1,)), ((), ())), preferred_element_type=F32)
    half = PEER_KEY_DIM // 2
    for hp in range(2 * PEER_HEADS):
        blk = qt[hp * half:(hp + 1) * half, :].astype(BF16)
        st_ref[hp * N_KEYS:(hp + 1) * N_KEYS, :] = jnp.dot(sk_ref[hp], blk, preferred_element_type=F32)


def _merge(x_ctx, x_lat, mods3, h, bg, y_ctx, y_lat, gc, gn, conv_w, wco, wno, wo, g1, b1, wqt, sk):
    row = lambda w: pl.BlockSpec((ROW_TILE, w), lambda i: (i, 0))
    full = lambda *s: pl.BlockSpec(s, lambda i: (0,) * len(s))
    eights = ROW_TILE // 8
    n8 = N_TOK // 8
    return pl.pallas_call(
        _merge_kernel,
        grid=(N_TILES,),
        in_specs=[_ctx_rows(D_MODEL), _lat_rows(D_MODEL),
                  pl.BlockSpec((1, 1, 6 * D_MODEL), lambda i: (_tile_mod_row(i), 0, 0)),
                  row(CONV_WIDTH),
                  pl.BlockSpec((8, CONV_WIDTH), lambda i: (jnp.maximum(i * eights - 1, 0), 0)),
                  pl.BlockSpec((8, CONV_WIDTH), lambda i: (jnp.minimum((i + 1) * eights, n8 - 1), 0)),
                  row(CONV_WIDTH), _ctx_rows(NA_WIDTH), _lat_rows(NA_WIDTH), row(D_MODEL), row(D_MODEL),
                  full(CONV_K, CONV_WIDTH), full(CONV_WIDTH, D_MODEL), full(NA_WIDTH, D_MODEL),
                  full(D_MODEL, D_MODEL), full(1, D_MODEL), full(1, D_MODEL),
                  full(PEER_HEADS * PEER_KEY_DIM, D_MODEL),
                  full(2 * PEER_HEADS, N_KEYS, PEER_KEY_DIM // 2)],
        out_specs=[row(D_MODEL), row(D_MODEL),
                   pl.BlockSpec((2 * PEER_HEADS * N_KEYS, ROW_TILE), lambda i: (0, i))],
        out_shape=[jax.ShapeDtypeStruct((N_TOK, D_MODEL), F32),
                   jax.ShapeDtypeStruct((N_TOK, D_MODEL), BF16),
                   jax.ShapeDtypeStruct((2 * PEER_HEADS * N_KEYS, N_TOK), F32)],
        compiler_params=_cparams(("parallel",)),
        name="merge",
    )(x_ctx, x_lat, mods3, h, h, h, bg, y_ctx, y_lat, gc, gn, conv_w, wco, wno, wo, g1, b1, wqt, sk)


def _sort16_desc(v):
    v = list(v)
    n = len(v)
    k = 2
    while k <= n:
        j = k // 2
        while j >= 1:
            for i in range(n):
                m = i ^ j
                if m > i:
                    hi, lo = jnp.maximum(v[i], v[m]), jnp.minimum(v[i], v[m])
                    v[i], v[m] = (hi, lo) if (i & k) == 0 else (lo, hi)
            j //= 2
        k *= 2
    return v


def _merge_top16(x, y):
    n = len(x)
    keep = [jnp.maximum(x[i], y[n - 1 - i]) for i in range(n)]
    drop = [jnp.minimum(x[i], y[n - 1 - i]) for i in range(n)]
    while len(drop) > 1:
        drop = [jnp.maximum(drop[2 * i], drop[2 * i + 1]) for i in range(len(drop) // 2)]
    j = n // 2
    while j >= 1:
        for i in range(n):
            m = i ^ j
            if m > i:
                keep[i], keep[m] = jnp.maximum(keep[i], keep[m]), jnp.minimum(keep[i], keep[m])
        j //= 2
    return keep, drop[0]


def _select_kernel(st_ref, tau_ref, a_ref, m1_ref, zinv_ref, top_ref):
    n_chunks = N_KEYS // 8

    def one_list(hp, carry):
        base = pl.multiple_of(hp * N_KEYS, N_KEYS)
        cur = _sort16_desc([st_ref[pl.ds(base + 8 * c, 8), :] for c in range(n_chunks)])
        rest = None
        for shift in (4, 2, 1):
            cur, dropped = _merge_top16(cur, [pltpu.roll(x, shift, 0) for x in cur])
            if rest is not None:
                dropped = jnp.maximum(dropped, jnp.maximum(rest, pltpu.roll(rest, shift, 0)))
            rest = dropped
        for r in range(PEER_TOPK):
            top_ref[hp, r] = cur[r]
        top_ref[hp, PEER_TOPK] = rest
        return carry

    lax.fori_loop(0, 2 * PEER_HEADS, one_list, 0)

    sub = lax.broadcasted_iota(jnp.int32, (PEER_HEADS, SEL_TOK), 0)

    def packed(p, r):
        out = top_ref[p, r]
        for hd in range(1, PEER_HEADS):
            out = jnp.where(sub == hd, top_ref[2 * hd + p, r], out)
        return out

    a = [packed(0, r) for r in range(N_RANK)]
    b = [packed(1, r) for r in range(N_RANK)]
    cand = [a[r0] + b[r1] for (r0, r1) in STAIR]
    cand = cand + [jnp.full_like(cand[0], -jnp.inf)] * (-len(cand) % PEER_TOPK)
    blocks = [_sort16_desc(cand[i:i + PEER_TOPK]) for i in range(0, len(cand), PEER_TOPK)]
    rests = []
    while len(blocks) > 1:
        merged = []
        for i in range(0, len(blocks) - 1, 2):
            keep, dropped = _merge_top16(blocks[i], blocks[i + 1])
            merged.append(keep)
            rests.append(dropped)
        blocks = merged + blocks[len(blocks) - len(blocks) % 2:]
    top = blocks[0]
    c17 = rests[0]
    for x in rests[1:]:
        c17 = jnp.maximum(c17, x)
    z = jnp.ones_like(top[0])
    for x in top[1:]:
        z = z + jnp.exp(x - top[0])
    tau_ref[...] = 0.5 * (top[PEER_TOPK - 1] + c17)
    for r in range(PEER_TOPK):
        a_ref[r] = a[r]
    m1_ref[...] = b[0]
    zinv_ref[...] = 1.0 / z


def _select(st):
    n_tok = st.shape[1]
    stat = pl.BlockSpec((PEER_HEADS, SEL_TOK), lambda i: (0, i))
    shp = jax.ShapeDtypeStruct((PEER_HEADS, n_tok), F32)
    return pl.pallas_call(
        _select_kernel,
        grid=(n_tok // SEL_TOK,),
        in_specs=[pl.BlockSpec((2 * PEER_HEADS * N_KEYS, SEL_TOK), lambda i: (0, i))],
        out_specs=[stat, pl.BlockSpec((PEER_TOPK, PEER_HEADS, SEL_TOK), lambda i: (0, 0, i)), stat, stat],
        out_shape=[shp, jax.ShapeDtypeStruct((PEER_TOPK, PEER_HEADS, n_tok), F32), shp, shp],
        scratch_shapes=[pltpu.VMEM((2 * PEER_HEADS, N_RANK, 8, SEL_TOK), F32)],
        compiler_params=_cparams(("parallel",)),
        name="select",
    )(st)


def _gelu(x):
    return 0.5 * x * (1.0 + lax.erf(x * np.float32(np.sqrt(0.5))))


def _peer_kernel(u2_ref, st_ref, tau_ref, a_ref, m1_ref, zinv_ref, u_ref, vt_ref,
                 x1_ref, mod_ref, g2_ref, b2_ref, o_ref,
                 rank0_ref, e0_ref, adm1_ref, e1_ref, s_ref, p_ref, acc_ref):
    eb = pl.program_id(1)

    @pl.when(eb == 0)
    def _prologue():
        half = N_KEYS // 2
        for hd in range(PEER_HEADS):
            for lane0 in range(0, PEER_TOK, 128):
                lanes = slice(lane0, lane0 + 128)
                a = [a_ref[r, hd:hd + 1, lanes] for r in range(PEER_TOPK)]
                tau = tau_ref[hd:hd + 1, lanes]
                for k0 in (0, half):
                    s0 = st_ref[(2 * hd) * N_KEYS + k0:(2 * hd) * N_KEYS + k0 + half, lanes]
                    s1 = st_ref[(2 * hd + 1) * N_KEYS + k0:(2 * hd + 1) * N_KEYS + k0 + half, lanes]
                    need = tau - s1
                    rank0 = jnp.zeros_like(s0)
                    adm1 = jnp.zeros_like(s1)
                    for r in range(PEER_TOPK):
                        rank0 = jnp.where(a[r] > s0, float(r + 1), rank0)
                        adm1 = jnp.where(a[r] >= need, float(r + 1), adm1)
                    rows = slice(hd * N_KEYS + k0, hd * N_KEYS + k0 + half)
                    rank0_ref[rows, lanes] = rank0
                    adm1_ref[rows, lanes] = adm1.astype(BF16)
                    e0_ref[rows, lanes] = jnp.exp(s0 - a[0]) * zinv_ref[hd:hd + 1, lanes]
                    e1_ref[rows, lanes] = jnp.exp(s1 - m1_ref[hd:hd + 1, lanes]).astype(BF16)
        acc_ref[...] = jnp.zeros_like(acc_ref)

    s_ref[...] = lax.dot_general(u_ref[...], u2_ref[...], (((1,), (1,)), ((), ())),
                                 preferred_element_type=F32)

    def gate_rows(r, carry):
        rows = pl.ds(pl.multiple_of(r * N_KEYS, N_KEYS), N_KEYS)
        key0 = eb * PEER_ROWS_PER_EB + r
        for lane0 in range(0, PEER_TOK, PEER_LANE):
            lanes = slice(lane0, lane0 + PEER_LANE)
            w = jnp.zeros((N_KEYS, PEER_LANE), BF16)
            for hd in range(PEER_HEADS):
                adm1 = adm1_ref[hd * N_KEYS:(hd + 1) * N_KEYS, lanes]
                e1 = e1_ref[hd * N_KEYS:(hd + 1) * N_KEYS, lanes]
                rank0 = rank0_ref[pl.ds(hd * N_KEYS + key0, 1), lanes].astype(BF16)
                e0 = e0_ref[pl.ds(hd * N_KEYS + key0, 1), lanes].astype(BF16)
                w = w + jnp.where(rank0 < adm1, e1, jnp.zeros_like(e1)) * e0
            p_ref[rows, lanes] = w * _gelu(s_ref[rows, lanes]).astype(BF16)
        return carry

    lax.fori_loop(0, PEER_ROWS_PER_EB, gate_rows, 0)

    acc_ref[...] += jnp.dot(vt_ref[...], p_ref[...], preferred_element_type=F32)

    @pl.when(eb == pl.num_programs(1) - 1)
    def _epilogue():
        y = acc_ref[...].T
        ga2 = mod_ref[0, :, 5 * D_MODEL:6 * D_MODEL]
        o_ref[...] = _layer_norm(ALPHA * x1_ref[...] + ga2 * y, g2_ref[...], b2_ref[...])


def _peer(u2, st, tau, a_sorted, m1, zinv, u_bf, vt_bf, x1, mods3, g2, b2, latent):
    n_tok = N_LAT if latent else N_CTX
    t0 = (N_CTX if latent else 0) // PEER_TOK
    tiles_per_seq = DEC_SEQ // PEER_TOK
    mod_row = (lambda t: 1 + t // tiles_per_seq) if latent else (lambda t: 0)
    stat = pl.BlockSpec((PEER_HEADS, PEER_TOK), lambda t, e: (0, t0 + t))
    return pl.pallas_call(
        _peer_kernel,
        grid=(n_tok // PEER_TOK, N_EXPERTS // PEER_EB),
        in_specs=[pl.BlockSpec((PEER_TOK, D_MODEL), lambda t, e: (t0 + t, 0)),
                  pl.BlockSpec((2 * PEER_HEADS * N_KEYS, PEER_TOK), lambda t, e: (0, t0 + t)),
                  stat,
                  pl.BlockSpec((PEER_TOPK, PEER_HEADS, PEER_TOK), lambda t, e: (0, 0, t0 + t)),
                  stat, stat,
                  pl.BlockSpec((PEER_EB, D_MODEL), lambda t, e: (e, 0)),
                  pl.BlockSpec((D_MODEL, PEER_EB), lambda t, e: (0, e)),
                  pl.BlockSpec((PEER_TOK, D_MODEL), lambda t, e: (t0 + t, 0)),
                  pl.BlockSpec((1, 1, 6 * D_MODEL), lambda t, e: (mod_row(t), 0, 0)),
                  pl.BlockSpec((1, D_MODEL), lambda t, e: (0, 0)),
                  pl.BlockSpec((1, D_MODEL), lambda t, e: (0, 0))],
        out_specs=pl.BlockSpec((PEER_TOK, D_MODEL), lambda t, e: (t, 0)),
        out_shape=jax.ShapeDtypeStruct((n_tok, D_MODEL), F32),
        scratch_shapes=[pltpu.VMEM((PEER_HEADS * N_KEYS, PEER_TOK), F32),
                        pltpu.VMEM((PEER_HEADS * N_KEYS, PEER_TOK), F32),
                        pltpu.VMEM((PEER_HEADS * N_KEYS, PEER_TOK), BF16),
                        pltpu.VMEM((PEER_HEADS * N_KEYS, PEER_TOK), BF16),
                        pltpu.VMEM((PEER_EB, PEER_TOK), F32),
                        pltpu.VMEM((PEER_EB, PEER_TOK), BF16),
                        pltpu.VMEM((D_MODEL, PEER_TOK), F32)],
        compiler_params=_cparams(("parallel", "arbitrary")),
        name="peer",
    )(u2, st, tau, a_sorted, m1, zinv, u_bf, vt_bf, x1, mods3, g2, b2)


def kernel(x_prompt, x_sample, cache_k, cache_v, c, c_ctx, w_ada, b_ada, w_in, conv_w, w_conv_out,
           w_na_out, rpb, w_o, ln1_g, ln1_b, w_query, sub_keys, expert_u, expert_v, ln2_g, ln2_b):
    l = 0
    x_ctx = x_prompt.reshape(N_CTX, D_MODEL)
    x_lat = x_sample.reshape(N_LAT, D_MODEL)
    cond = jnp.concatenate([c_ctx[None, :], c, jnp.zeros((N_COND - 1 - DEC_BATCH, D_MODEL), F32)], axis=0)

    mods = _mods(cond, w_ada[l], b_ada[l])
    mods3 = mods.reshape(N_COND, 1, 6 * D_MODEL)

    h, bg, q, k, v, gc, gn = _inproj(x_ctx, x_lat, mods3, w_in[l].astype(BF16))

    y_ctx = _ctx_attn(q, k, v)
    k_ctx = cache_k[:, l].reshape(DEC_BATCH, PAST_LEN, NA_WIDTH)
    v_ctx = cache_v[:, l].reshape(DEC_BATCH, PAST_LEN, NA_WIDTH)
    y_lat = _lat_attn(q, k, v, k_ctx, v_ctx, _window_bias(rpb[l]))

    wqt = w_query[l].T.astype(BF16)
    sk = sub_keys[l].reshape(2 * PEER_HEADS, N_KEYS, PEER_KEY_DIM // 2).astype(BF16)
    x1, u2, st = _merge(x_ctx, x_lat, mods3, h, bg, y_ctx, y_lat, gc, gn, conv_w[l],
                        w_conv_out[l].astype(BF16), w_na_out[l].astype(BF16), w_o[l].astype(BF16),
                        ln1_g[l].reshape(1, D_MODEL), ln1_b[l].reshape(1, D_MODEL), wqt, sk)

    tau, a_sorted, m1, zinv = _select(st)
    peer_args = (u2, st, tau, a_sorted, m1, zinv, expert_u[l].astype(BF16), expert_v[l].T.astype(BF16),
                 x1, mods3, ln2_g[l].reshape(1, D_MODEL), ln2_b[l].reshape(1, D_MODEL))
    y_prompt = _peer(*peer_args, latent=False).reshape(BATCH, SEQ, D_MODEL)
    y_sample = _peer(*peer_args, latent=True).reshape(DEC_BATCH, DEC_SEQ, D_MODEL)
    new_k = k[:N_CTX].reshape(BATCH, 1, SEQ, NA_HEADS, NA_HEAD_DIM)
    new_v = v[:N_CTX].reshape(BATCH, 1, SEQ, NA_HEADS, NA_HEAD_DIM)
    return (y_prompt, y_sample, new_k, new_v)
```

```python
import functools

import numpy as np
import jax
import jax.numpy as jnp
from jax import lax
from jax.experimental import pallas as pl
from jax.experimental.pallas import tpu as pltpu

F32 = jnp.float32
BF16 = jnp.bfloat16

D_MODEL = 1024
BATCH = 16
SEQ = 256
DEC_BATCH = 8
DEC_SEQ = 1024
PAST_LEN = 512
GRID_W = 64
GRID_ROWS = DEC_SEQ // GRID_W
CONV_WIDTH = 512
CONV_K = 3
NA_HEADS = 8
NA_HEAD_DIM = 64
NA_WIDTH = NA_HEADS * NA_HEAD_DIM
WIN_H = 8
WIN_W = 16
PEER_HEADS = 8
PEER_KEY_DIM = 256
N_KEYS = 128
N_EXPERTS = N_KEYS * N_KEYS
PEER_TOPK = 16
LN_EPS = 1e-5
NEG_INF = -1e30
DEPTH = 1
ALPHA = (2 * DEPTH) ** 0.25
IN_COLS = 3 * CONV_WIDTH + 3 * NA_WIDTH + 2 * D_MODEL

N_CTX = BATCH * SEQ
N_LAT = DEC_BATCH * DEC_SEQ
N_TOK = N_CTX + N_LAT
N_COND = 16

ROW_TILE = 256
CTX_TILES = N_CTX // ROW_TILE
LAT_TILES_PER_SEQ = DEC_SEQ // ROW_TILE
N_TILES = N_TOK // ROW_TILE

PEER_TOK = 512
PEER_LANE = 256
PEER_EB = 2048
PEER_ROWS_PER_EB = PEER_EB // N_KEYS
SEL_TOK = 256
LAT_QB = 256

VMEM_LIMIT = 56 * 1024 * 1024

N_RANK = PEER_TOPK + 1
STAIR = [(a, b) for a in range(N_RANK) for b in range(N_RANK) if (a + 1) * (b + 1) <= N_RANK]


def _cparams(sem):
    return pltpu.CompilerParams(dimension_semantics=sem, vmem_limit_bytes=VMEM_LIMIT)


def _tile_mod_row(i):
    return jnp.where(i < CTX_TILES, 0, 1 + (i - CTX_TILES) // LAT_TILES_PER_SEQ)


def _mods_kernel(cond_ref, w_ref, b_ref, o_ref):
    cnd = cond_ref[...]
    act = cnd * jax.nn.sigmoid(cnd)
    o_ref[...] = jnp.dot(act, w_ref[...], precision=lax.Precision.HIGHEST,
                         preferred_element_type=F32) + b_ref[...]


def _mods(cond, w_ada, b_ada):
    cols = 6 * D_MODEL
    blk = 1536
    return pl.pallas_call(
        _mods_kernel,
        grid=(cols // blk,),
        in_specs=[pl.BlockSpec((N_COND, D_MODEL), lambda j: (0, 0)),
                  pl.BlockSpec((D_MODEL, blk), lambda j: (0, j)),
                  pl.BlockSpec((1, blk), lambda j: (0, j))],
        out_specs=pl.BlockSpec((N_COND, blk), lambda j: (0, j)),
        out_shape=jax.ShapeDtypeStruct((N_COND, cols), F32),
        compiler_params=_cparams(("arbitrary",)),
        name="mods",
    )(cond, w_ada, b_ada.reshape(1, cols))


def _ctx_rows(width):
    return pl.BlockSpec((ROW_TILE, width), lambda i: (jnp.minimum(i, CTX_TILES - 1), 0))


def _lat_rows(width):
    return pl.BlockSpec((ROW_TILE, width), lambda i: (jnp.maximum(i - CTX_TILES, 0), 0))


def _tile_of(ctx_ref, lat_ref):
    return jnp.where(pl.program_id(0) < CTX_TILES, ctx_ref[...], lat_ref[...])


def _inproj_kernel(xc_ref, xl_ref, mod_ref, w_ref, h_ref, bg_ref, q_ref, k_ref, v_ref, gc_ref, gn_ref):
    sh1 = mod_ref[0, :, 0:D_MODEL]
    sc1 = mod_ref[0, :, D_MODEL:2 * D_MODEL]
    u = (_tile_of(xc_ref, xl_ref) * (1 + sc1) + sh1).astype(BF16)

    def proj(lo, hi):
        return jnp.dot(u, w_ref[:, lo:hi], preferred_element_type=F32)

    c = CONV_WIDTH
    a = NA_WIDTH
    xc = proj(0, c)
    cg = proj(2 * c, 3 * c)
    h_ref[...] = cg * xc
    bg_ref[...] = proj(c, 2 * c)
    o = 3 * c
    q_ref[...] = (proj(o, o + a) * (NA_HEAD_DIM ** -0.5)).astype(BF16)
    k_ref[...] = proj(o + a, o + 2 * a)
    v_ref[...] = proj(o + 2 * a, o + 3 * a)
    o = o + 3 * a
    gc_ref[...] = jax.nn.sigmoid(proj(o, o + D_MODEL))
    gn_ref[...] = jax.nn.sigmoid(proj(o + D_MODEL, o + 2 * D_MODEL))


def _inproj(x_ctx, x_lat, mods3, w_in_bf):
    row = lambda w: pl.BlockSpec((ROW_TILE, w), lambda i: (i, 0))
    return pl.pallas_call(
        _inproj_kernel,
        grid=(N_TILES,),
        in_specs=[_ctx_rows(D_MODEL), _lat_rows(D_MODEL),
                  pl.BlockSpec((1, 1, 6 * D_MODEL), lambda i: (_tile_mod_row(i), 0, 0)),
                  pl.BlockSpec((D_MODEL, IN_COLS), lambda i: (0, 0))],
        out_specs=[row(CONV_WIDTH), row(CONV_WIDTH), row(NA_WIDTH), row(NA_WIDTH), row(NA_WIDTH),
                   row(D_MODEL), row(D_MODEL)],
        out_shape=[jax.ShapeDtypeStruct((N_TOK, CONV_WIDTH), F32),
                   jax.ShapeDtypeStruct((N_TOK, CONV_WIDTH), F32),
                   jax.ShapeDtypeStruct((N_TOK, NA_WIDTH), BF16),
                   jax.ShapeDtypeStruct((N_TOK, NA_WIDTH), F32),
                   jax.ShapeDtypeStruct((N_TOK, NA_WIDTH), F32),
                   jax.ShapeDtypeStruct((N_TOK, D_MODEL), F32),
                   jax.ShapeDtypeStruct((N_TOK, D_MODEL), F32)],
        compiler_params=_cparams(("parallel",)),
        name="inproj",
    )(x_ctx, x_lat, mods3, w_in_bf)


def _head_pair_attention(q2, kv_list, bias_list):
    lane = lax.broadcasted_iota(jnp.int32, q2.shape, 1)
    outs = []
    for hh in range(2):
        in_head = (lane >= hh * NA_HEAD_DIM) & (lane < (hh + 1) * NA_HEAD_DIM)
        qh = jnp.where(in_head, q2, jnp.zeros_like(q2))
        scores = []
        for (k2, _), bias in zip(kv_list, bias_list):
            s = lax.dot_general(qh, k2, (((1,), (1,)), ((), ())), preferred_element_type=F32)
            if bias is not None:
                s = s + bias(hh)
            scores.append(s)
        m = scores[0].max(axis=-1, keepdims=True)
        for s in scores[1:]:
            m = jnp.maximum(m, s.max(axis=-1, keepdims=True))
        acc = None
        den = None
        for s, (_, v2) in zip(scores, kv_list):
            p = jnp.exp(s - m)
            d = p.sum(axis=-1, keepdims=True)
            o = jnp.dot(p.astype(BF16), v2, preferred_element_type=F32)
            acc = o if acc is None else acc + o
            den = d if den is None else den + d
        outs.append(acc / den)
    return jnp.where(lane < NA_HEAD_DIM, outs[0], outs[1])


def _ctx_attn_kernel(q_ref, k_ref, v_ref, o_ref):
    k2 = k_ref[...].astype(BF16)
    v2 = v_ref[...].astype(BF16)
    o_ref[...] = _head_pair_attention(q_ref[...], [(k2, v2)], [None]).astype(BF16)


def _ctx_attn(q, k, v):
    spec = pl.BlockSpec((SEQ, 128), lambda b, p: (b, p))
    return pl.pallas_call(
        _ctx_attn_kernel,
        grid=(BATCH, NA_WIDTH // 128),
        in_specs=[spec, spec, spec],
        out_specs=spec,
        out_shape=jax.ShapeDtypeStruct((N_CTX, NA_WIDTH), BF16),
        compiler_params=_cparams(("parallel", "parallel")),
        name="ctx_attn",
    )(q, k, v)


def _lat_attn_kernel(q_ref, k_ref, v_ref, kc_ref, vc_ref, cols_ref, o_ref, bias_ref):
    @pl.when(pl.program_id(2) == 0)
    def _build_bias():
        kh = min(WIN_H, GRID_ROWS)
        for lr in range(LAT_QB // GRID_W):
            qr = pl.program_id(0) * (LAT_QB // GRID_W) + lr
            row_start = jnp.clip(qr - kh // 2, 0, GRID_ROWS - kh)
            for kr in range(GRID_ROWS):
                in_rows = (kr >= row_start) & (kr < row_start + kh)
                dr = jnp.clip(kr - qr + (WIN_H - 1), 0, 2 * WIN_H - 2)
                for hh in range(2):
                    tile = jnp.where(in_rows, cols_ref[hh, dr], NEG_INF)
                    bias_ref[hh, lr * GRID_W:(lr + 1) * GRID_W, kr * GRID_W:(kr + 1) * GRID_W] = tile

    kw = k_ref[...].astype(BF16)
    vw = v_ref[...].astype(BF16)
    kc = kc_ref[0].astype(BF16)
    vc = vc_ref[0].astype(BF16)
    o = _head_pair_attention(q_ref[...], [(kw, vw), (kc, vc)], [lambda hh: bias_ref[hh], None])
    o_ref[...] = o.astype(BF16)


def _lat_attn(q, k, v, k_ctx, v_ctx, cols):
    n_qb = DEC_SEQ // LAT_QB
    q_off = N_CTX // LAT_QB
    kv_off = N_CTX // DEC_SEQ
    return pl.pallas_call(
        _lat_attn_kernel,
        grid=(n_qb, NA_WIDTH // 128, DEC_BATCH),
        in_specs=[pl.BlockSpec((LAT_QB, 128), lambda qb, p, b: (q_off + b * n_qb + qb, p)),
                  pl.BlockSpec((DEC_SEQ, 128), lambda qb, p, b: (kv_off + b, p)),
                  pl.BlockSpec((DEC_SEQ, 128), lambda qb, p, b: (kv_off + b, p)),
                  pl.BlockSpec((1, PAST_LEN, 128), lambda qb, p, b: (b, 0, p)),
                  pl.BlockSpec((1, PAST_LEN, 128), lambda qb, p, b: (b, 0, p)),
                  pl.BlockSpec((2, 2 * WIN_H - 1, GRID_W, GRID_W), lambda qb, p, b: (p, 0, 0, 0))],
        out_specs=pl.BlockSpec((LAT_QB, 128), lambda qb, p, b: (b * n_qb + qb, p)),
        out_shape=jax.ShapeDtypeStruct((N_LAT, NA_WIDTH), BF16),
        scratch_shapes=[pltpu.VMEM((2, LAT_QB, DEC_SEQ), F32)],
        compiler_params=_cparams(("parallel", "parallel", "arbitrary")),
        name="lat_attn",
    )(q, k, v, k_ctx, v_ctx, cols)


def _window_cols(rpb):
    col = np.arange(GRID_W)
    col_start = np.clip(col - WIN_W // 2, 0, GRID_W - WIN_W)
    col_in = (col[None, :] >= col_start[:, None]) & (col[None, :] < col_start[:, None] + WIN_W)
    dc = np.clip(col[None, :] - col[:, None], -(WIN_W - 1), WIN_W - 1) + (WIN_W - 1)
    onehot = (np.arange(2 * WIN_W - 1)[:, None, None] == dc[None]).astype(np.float32)
    cols = jnp.einsum('hrd,dqk->hrqk', rpb.astype(F32), jnp.asarray(onehot), precision=lax.Precision.HIGHEST)
    return jnp.where(jnp.asarray(col_in)[None, None], cols, NEG_INF)


def _layer_norm(x, g, b):
    mu = jnp.mean(x, axis=-1, keepdims=True)
    xc = x - mu
    var = jnp.mean(xc * xc, axis=-1, keepdims=True)
    return xc * lax.rsqrt(var + LN_EPS) * g + b


def _merge_kernel(xc_ref, xl_ref, mod_ref, h_ref, hp_ref, hn_ref, bg_ref, ync_ref, ynl_ref, gc_ref, gn_ref,
                  cw_ref, wco_ref, wno_ref, wo_ref, g1_ref, b1_ref, wqt_ref, sk_ref,
                  x1_ref, u2_ref, st_ref):
    i = pl.program_id(0)
    pos = (i - CTX_TILES) % LAT_TILES_PER_SEQ
    is_ctx = i < CTX_TILES
    has_prev = jnp.where(is_ctx | (pos == 0), 0.0, 1.0).astype(F32)
    has_next = jnp.where(is_ctx | (pos == LAT_TILES_PER_SEQ - 1), 0.0, 1.0).astype(F32)

    h = h_ref[...]
    rows = lax.broadcasted_iota(jnp.int32, h.shape, 0)
    prev_row = hp_ref[7:8, :] * has_prev
    next_row = hn_ref[0:1, :] * has_next
    h_prev = jnp.where(rows == 0, prev_row, pltpu.roll(h, 1, 0))
    h_next = jnp.where(rows == ROW_TILE - 1, next_row, pltpu.roll(h, ROW_TILE - 1, 0))
    y = cw_ref[0:1, :] * h_prev
    y = y + cw_ref[1:2, :] * h
    y = y + cw_ref[2:3, :] * h_next
    y_conv = (bg_ref[...] * y).astype(BF16)

    merged = (gc_ref[...] * jnp.dot(y_conv, wco_ref[...], preferred_element_type=F32) +
              gn_ref[...] * jnp.dot(_tile_of(ync_ref, ynl_ref), wno_ref[...],
                                    preferred_element_type=F32))
    z = jnp.dot(merged.astype(BF16), wo_ref[...], preferred_element_type=F32)

    ga1 = mod_ref[0, :, 2 * D_MODEL:3 * D_MODEL]
    sh2 = mod_ref[0, :, 3 * D_MODEL:4 * D_MODEL]
    sc2 = mod_ref[0, :, 4 * D_MODEL:5 * D_MODEL]
    x1 = _layer_norm(ALPHA * _tile_of(xc_ref, xl_ref) + ga1 * z, g1_ref[...], b1_ref[...])
    x1_ref[...] = x1
    u2 = (x1 * (1 + sc2) + sh2).astype(BF16)
    u2_ref[...] = u2

    qt = lax.dot_general(wqt_ref[...], u2, (((1,), (1,)), ((), ())), preferred_element_type=F32)
    half = PEER_KEY_DIM // 2
    for hp in range(2 * PEER_HEADS):
        blk = qt[hp * half:(hp + 1) * half, :].astype(BF16)
        st_ref[hp * N_KEYS:(hp + 1) * N_KEYS, :] = jnp.dot(sk_ref[hp], blk, preferred_element_type=F32)


def _merge(x_ctx, x_lat, mods3, h, bg, y_ctx, y_lat, gc, gn, conv_w, wco, wno, wo, g1, b1, wqt, sk):
    row = lambda w: pl.BlockSpec((ROW_TILE, w), lambda i: (i, 0))
    full = lambda *s: pl.BlockSpec(s, lambda i: (0,) * len(s))
    eights = ROW_TILE // 8
    n8 = N_TOK // 8
    return pl.pallas_call(
        _merge_kernel,
        grid=(N_TILES,),
        in_specs=[_ctx_rows(D_MODEL), _lat_rows(D_MODEL),
                  pl.BlockSpec((1, 1, 6 * D_MODEL), lambda i: (_tile_mod_row(i), 0, 0)),
                  row(CONV_WIDTH),
                  pl.BlockSpec((8, CONV_WIDTH), lambda i: (jnp.maximum(i * eights - 1, 0), 0)),
                  pl.BlockSpec((8, CONV_WIDTH), lambda i: (jnp.minimum((i + 1) * eights, n8 - 1), 0)),
                  row(CONV_WIDTH), _ctx_rows(NA_WIDTH), _lat_rows(NA_WIDTH), row(D_MODEL), row(D_MODEL),
                  full(CONV_K, CONV_WIDTH), full(CONV_WIDTH, D_MODEL), full(NA_WIDTH, D_MODEL),
                  full(D_MODEL, D_MODEL), full(1, D_MODEL), full(1, D_MODEL),
                  full(PEER_HEADS * PEER_KEY_DIM, D_MODEL),
                  full(2 * PEER_HEADS, N_KEYS, PEER_KEY_DIM // 2)],
        out_specs=[row(D_MODEL), row(D_MODEL),
                   pl.BlockSpec((2 * PEER_HEADS * N_KEYS, ROW_TILE), lambda i: (0, i))],
        out_shape=[jax.ShapeDtypeStruct((N_TOK, D_MODEL), F32),
                   jax.ShapeDtypeStruct((N_TOK, D_MODEL), BF16),
                   jax.ShapeDtypeStruct((2 * PEER_HEADS * N_KEYS, N_TOK), F32)],
        compiler_params=_cparams(("parallel",)),
        name="merge",
    )(x_ctx, x_lat, mods3, h, h, h, bg, y_ctx, y_lat, gc, gn, conv_w, wco, wno, wo, g1, b1, wqt, sk)


def _sort16_desc(v):
    v = list(v)
    n = len(v)
    k = 2
    while k <= n:
        j = k // 2
        while j >= 1:
            for i in range(n):
                m = i ^ j
                if m > i:
                    hi, lo = jnp.maximum(v[i], v[m]), jnp.minimum(v[i], v[m])
                    v[i], v[m] = (hi, lo) if (i & k) == 0 else (lo, hi)
            j //= 2
        k *= 2
    return v


def _merge_top16(x, y):
    n = len(x)
    keep = [jnp.maximum(x[i], y[n - 1 - i]) for i in range(n)]
    drop = [jnp.minimum(x[i], y[n - 1 - i]) for i in range(n)]
    while len(drop) > 1:
        drop = [jnp.maximum(drop[2 * i], drop[2 * i + 1]) for i in range(len(drop) // 2)]
    j = n // 2
    while j >= 1:
        for i in range(n):
            m = i ^ j
            if m > i:
                keep[i], keep[m] = jnp.maximum(keep[i], keep[m]), jnp.minimum(keep[i], keep[m])
        j //= 2
    return keep, drop[0]


def _select_kernel(st_ref, tau_ref, a_ref, m1_ref, zinv_ref, top_ref):
    n_chunks = N_KEYS // 8

    def one_list(hp, carry):
        base = pl.multiple_of(hp * N_KEYS, N_KEYS)
        cur = _sort16_desc([st_ref[pl.ds(base + 8 * c, 8), :] for c in range(n_chunks)])
        rest = None
        for shift in (4, 2, 1):
            cur, dropped = _merge_top16(cur, [pltpu.roll(x, shift, 0) for x in cur])
            if rest is not None:
                dropped = jnp.maximum(dropped, jnp.maximum(rest, pltpu.roll(rest, shift, 0)))
            rest = dropped
        for r in range(PEER_TOPK):
            top_ref[hp, r] = cur[r]
        top_ref[hp, PEER_TOPK] = rest
        return carry

    lax.fori_loop(0, 2 * PEER_HEADS, one_list, 0)

    sub = lax.broadcasted_iota(jnp.int32, (PEER_HEADS, SEL_TOK), 0)

    def packed(p, r):
        out = top_ref[p, r]
        for hd in range(1, PEER_HEADS):
            out = jnp.where(sub == hd, top_ref[2 * hd + p, r], out)
        return out

    a = [packed(0, r) for r in range(N_RANK)]
    b = [packed(1, r) for r in range(N_RANK)]
    cand = [a[r0] + b[r1] for (r0, r1) in STAIR]
    cand = cand + [jnp.full_like(cand[0], -jnp.inf)] * (-len(cand) % PEER_TOPK)
    blocks = [_sort16_desc(cand[i:i + PEER_TOPK]) for i in range(0, len(cand), PEER_TOPK)]
    rests = []
    while len(blocks) > 1:
        merged = []
        for i in range(0, len(blocks) - 1, 2):
            keep, dropped = _merge_top16(blocks[i], blocks[i + 1])
            merged.append(keep)
            rests.append(dropped)
        blocks = merged + blocks[len(blocks) - len(blocks) % 2:]
    top = blocks[0]
    c17 = rests[0]
    for x in rests[1:]:
        c17 = jnp.maximum(c17, x)
    z = jnp.ones_like(top[0])
    for x in top[1:]:
        z = z + jnp.exp(x - top[0])
    tau_ref[...] = 0.5 * (top[PEER_TOPK - 1] + c17)
    for r in range(PEER_TOPK):
        a_ref[r] = a[r]
    m1_ref[...] = b[0]
    zinv_ref[...] = 1.0 / z


def _select(st):
    n_tok = st.shape[1]
    stat = pl.BlockSpec((PEER_HEADS, SEL_TOK), lambda i: (0, i))
    shp = jax.ShapeDtypeStruct((PEER_HEADS, n_tok), F32)
    return pl.pallas_call(
        _select_kernel,
        grid=(n_tok // SEL_TOK,),
        in_specs=[pl.BlockSpec((2 * PEER_HEADS * N_KEYS, SEL_TOK), lambda i: (0, i))],
        out_specs=[stat, pl.BlockSpec((PEER_TOPK, PEER_HEADS, SEL_TOK), lambda i: (0, 0, i)), stat, stat],
        out_shape=[shp, jax.ShapeDtypeStruct((PEER_TOPK, PEER_HEADS, n_tok), F32), shp, shp],
        scratch_shapes=[pltpu.VMEM((2 * PEER_HEADS, N_RANK, 8, SEL_TOK), F32)],
        compiler_params=_cparams(("parallel",)),
        name="select",
    )(st)


def _gelu(x):
    return 0.5 * x * (1.0 + lax.erf(x * np.float32(np.sqrt(0.5))))


def _peer_kernel(u2_ref, st_ref, tau_ref, a_ref, m1_ref, zinv_ref, u_ref, vt_ref,
                 x1_ref, mod_ref, g2_ref, b2_ref, o_ref,
                 rank0_ref, e0_ref, adm1_ref, e1_ref, s_ref, p_ref, acc_ref):
    eb = pl.program_id(1)

    @pl.when(eb == 0)
    def _prologue():
        half = N_KEYS // 2
        for hd in range(PEER_HEADS):
            for lane0 in range(0, PEER_TOK, 128):
                lanes = slice(lane0, lane0 + 128)
                a = [a_ref[r, hd:hd + 1, lanes] for r in range(PEER_TOPK)]
                tau = tau_ref[hd:hd + 1, lanes]
                for k0 in (0, half):
                    s0 = st_ref[(2 * hd) * N_KEYS + k0:(2 * hd) * N_KEYS + k0 + half, lanes]
                    s1 = st_ref[(2 * hd + 1) * N_KEYS + k0:(2 * hd + 1) * N_KEYS + k0 + half, lanes]
                    need = tau - s1
                    rank0 = jnp.zeros_like(s0)
                    adm1 = jnp.zeros_like(s1)
                    for r in range(PEER_TOPK):
                        rank0 = jnp.where(a[r] > s0, float(r + 1), rank0)
                        adm1 = jnp.where(a[r] >= need, float(r + 1), adm1)
                    rows = slice(hd * N_KEYS + k0, hd * N_KEYS + k0 + half)
                    rank0_ref[rows, lanes] = rank0
                    adm1_ref[rows, lanes] = adm1.astype(BF16)
                    e0_ref[rows, lanes] = jnp.exp(s0 - a[0]) * zinv_ref[hd:hd + 1, lanes]
                    e1_ref[rows, lanes] = jnp.exp(s1 - m1_ref[hd:hd + 1, lanes]).astype(BF16)
        acc_ref[...] = jnp.zeros_like(acc_ref)

    s_ref[...] = lax.dot_general(u_ref[...], u2_ref[...], (((1,), (1,)), ((), ())),
                                 preferred_element_type=F32)

    def gate_rows(r, carry):
        rows = pl.ds(pl.multiple_of(r * N_KEYS, N_KEYS), N_KEYS)
        key0 = eb * PEER_ROWS_PER_EB + r
        for lane0 in range(0, PEER_TOK, PEER_LANE):
            lanes = slice(lane0, lane0 + PEER_LANE)
            w = jnp.zeros((N_KEYS, PEER_LANE), BF16)
            for hd in range(PEER_HEADS):
                adm1 = adm1_ref[hd * N_KEYS:(hd + 1) * N_KEYS, lanes]
                e1 = e1_ref[hd * N_KEYS:(hd + 1) * N_KEYS, lanes]
                rank0 = rank0_ref[pl.ds(hd * N_KEYS + key0, 1), lanes].astype(BF16)
                e0 = e0_ref[pl.ds(hd * N_KEYS + key0, 1), lanes].astype(BF16)
                w = w + jnp.where(rank0 < adm1, e1, jnp.zeros_like(e1)) * e0
            p_ref[rows, lanes] = w * _gelu(s_ref[rows, lanes]).astype(BF16)
        return carry

    lax.fori_loop(0, PEER_ROWS_PER_EB, gate_rows, 0)

    acc_ref[...] += jnp.dot(vt_ref[...], p_ref[...], preferred_element_type=F32)

    @pl.when(eb == pl.num_programs(1) - 1)
    def _epilogue():
        y = acc_ref[...].T
        ga2 = mod_ref[0, :, 5 * D_MODEL:6 * D_MODEL]
        o_ref[...] = _layer_norm(ALPHA * x1_ref[...] + ga2 * y, g2_ref[...], b2_ref[...])


def _peer(u2, st, tau, a_sorted, m1, zinv, u_bf, vt_bf, x1, mods3, g2, b2, latent):
    n_tok = N_LAT if latent else N_CTX
    t0 = (N_CTX if latent else 0) // PEER_TOK
    tiles_per_seq = DEC_SEQ // PEER_TOK
    mod_row = (lambda t: 1 + t // tiles_per_seq) if latent else (lambda t: 0)
    stat = pl.BlockSpec((PEER_HEADS, PEER_TOK), lambda t, e: (0, t0 + t))
    return pl.pallas_call(
        _peer_kernel,
        grid=(n_tok // PEER_TOK, N_EXPERTS // PEER_EB),
        in_specs=[pl.BlockSpec((PEER_TOK, D_MODEL), lambda t, e: (t0 + t, 0)),
                  pl.BlockSpec((2 * PEER_HEADS * N_KEYS, PEER_TOK), lambda t, e: (0, t0 + t)),
                  stat,
                  pl.BlockSpec((PEER_TOPK, PEER_HEADS, PEER_TOK), lambda t, e: (0, 0, t0 + t)),
                  stat, stat,
                  pl.BlockSpec((PEER_EB, D_MODEL), lambda t, e: (e, 0)),
                  pl.BlockSpec((D_MODEL, PEER_EB), lambda t, e: (0, e)),
                  pl.BlockSpec((PEER_TOK, D_MODEL), lambda t, e: (t0 + t, 0)),
                  pl.BlockSpec((1, 1, 6 * D_MODEL), lambda t, e: (mod_row(t), 0, 0)),
                  pl.BlockSpec((1, D_MODEL), lambda t, e: (0, 0)),
                  pl.BlockSpec((1, D_MODEL), lambda t, e: (0, 0))],
        out_specs=pl.BlockSpec((PEER_TOK, D_MODEL), lambda t, e: (t, 0)),
        out_shape=jax.ShapeDtypeStruct((n_tok, D_MODEL), F32),
        scratch_shapes=[pltpu.VMEM((PEER_HEADS * N_KEYS, PEER_TOK), F32),
                        pltpu.VMEM((PEER_HEADS * N_KEYS, PEER_TOK), F32),
                        pltpu.VMEM((PEER_HEADS * N_KEYS, PEER_TOK), BF16),
                        pltpu.VMEM((PEER_HEADS * N_KEYS, PEER_TOK), BF16),
                        pltpu.VMEM((PEER_EB, PEER_TOK), F32),
                        pltpu.VMEM((PEER_EB, PEER_TOK), BF16),
                        pltpu.VMEM((D_MODEL, PEER_TOK), F32)],
        compiler_params=_cparams(("parallel", "arbitrary")),
        name="peer",
    )(u2, st, tau, a_sorted, m1, zinv, u_bf, vt_bf, x1, mods3, g2, b2)


def kernel(x_prompt, x_sample, cache_k, cache_v, c, c_ctx, w_ada, b_ada, w_in, conv_w, w_conv_out,
           w_na_out, rpb, w_o, ln1_g, ln1_b, w_query, sub_keys, expert_u, expert_v, ln2_g, ln2_b):
    l = 0
    x_ctx = x_prompt.reshape(N_CTX, D_MODEL)
    x_lat = x_sample.reshape(N_LAT, D_MODEL)
    cond = jnp.concatenate([c_ctx[None, :], c, jnp.zeros((N_COND - 1 - DEC_BATCH, D_MODEL), F32)], axis=0)

    mods = _mods(cond, w_ada[l], b_ada[l])
    mods3 = mods.reshape(N_COND, 1, 6 * D_MODEL)

    h, bg, q, k, v, gc, gn = _inproj(x_ctx, x_lat, mods3, w_in[l].astype(BF16))

    y_ctx = _ctx_attn(q, k, v)
    k_ctx = cache_k[:, l].reshape(DEC_BATCH, PAST_LEN, NA_WIDTH)
    v_ctx = cache_v[:, l].reshape(DEC_BATCH, PAST_LEN, NA_WIDTH)
    y_lat = _lat_attn(q, k, v, k_ctx, v_ctx, _window_cols(rpb[l]))

    wqt = w_query[l].T.astype(BF16)
    sk = sub_keys[l].reshape(2 * PEER_HEADS, N_KEYS, PEER_KEY_DIM // 2).astype(BF16)
    x1, u2, st = _merge(x_ctx, x_lat, mods3, h, bg, y_ctx, y_lat, gc, gn, conv_w[l],
                        w_conv_out[l].astype(BF16), w_na_out[l].astype(BF16), w_o[l].astype(BF16),
                        ln1_g[l].reshape(1, D_MODEL), ln1_b[l].reshape(1, D_MODEL), wqt, sk)

    tau, a_sorted, m1, zinv = _select(st)
    peer_args = (u2, st, tau, a_sorted, m1, zinv, expert_u[l].astype(BF16), expert_v[l].T.astype(BF16),
                 x1, mods3, ln2_g[l].reshape(1, D_MODEL), ln2_b[l].reshape(1, D_MODEL))
    y_prompt = _peer(*peer_args, latent=False).reshape(BATCH, SEQ, D_MODEL)
    y_sample = _peer(*peer_args, latent=True).reshape(DEC_BATCH, DEC_SEQ, D_MODEL)
    new_k = k[:N_CTX].reshape(BATCH, 1, SEQ, NA_HEADS, NA_HEAD_DIM)
    new_v = v[:N_CTX].reshape(BATCH, 1, SEQ, NA_HEADS, NA_HEAD_DIM)
    return (y_prompt, y_sample, new_k, new_v)
```

```python
import functools

import numpy as np
import jax
import jax.numpy as jnp
from jax import lax
from jax.experimental import pallas as pl
from jax.experimental.pallas import tpu as pltpu

F32 = jnp.float32
BF16 = jnp.bfloat16

D_MODEL = 1024
BATCH = 16
SEQ = 256
DEC_BATCH = 8
DEC_SEQ = 1024
PAST_LEN = 512
GRID_W = 64
GRID_ROWS = DEC_SEQ // GRID_W
CONV_WIDTH = 512
CONV_K = 3
NA_HEADS = 8
NA_HEAD_DIM = 64
NA_WIDTH = NA_HEADS * NA_HEAD_DIM
WIN_H = 8
WIN_W = 16
PEER_HEADS = 8
PEER_KEY_DIM = 256
N_KEYS = 128
N_EXPERTS = N_KEYS * N_KEYS
PEER_TOPK = 16
LN_EPS = 1e-5
NEG_INF = -1e30
DEPTH = 1
ALPHA = (2 * DEPTH) ** 0.25
IN_COLS = 3 * CONV_WIDTH + 3 * NA_WIDTH + 2 * D_MODEL

N_CTX = BATCH * SEQ
N_LAT = DEC_BATCH * DEC_SEQ
N_TOK = N_CTX + N_LAT
N_COND = 16

ROW_TILE = 256
CTX_TILES = N_CTX // ROW_TILE
LAT_TILES_PER_SEQ = DEC_SEQ // ROW_TILE
N_TILES = N_TOK // ROW_TILE

PEER_TOK = 512
PEER_LANE = 256
PEER_HB = 1024
SEL_TOK = 256
LAT_QB = 256

VMEM_LIMIT = 60 * 1024 * 1024

N_RANK = PEER_TOPK + 1
STAIR = [(a, b) for a in range(N_RANK) for b in range(N_RANK) if (a + 1) * (b + 1) <= N_RANK]


def _cparams(sem):
    return pltpu.CompilerParams(dimension_semantics=sem, vmem_limit_bytes=VMEM_LIMIT)


def _tile_mod_row(i):
    return jnp.where(i < CTX_TILES, 0, 1 + (i - CTX_TILES) // LAT_TILES_PER_SEQ)


def _mods_kernel(cond_ref, w_ref, b_ref, o_ref):
    cnd = cond_ref[...]
    act = cnd * jax.nn.sigmoid(cnd)
    o_ref[...] = jnp.dot(act, w_ref[...], precision=lax.Precision.HIGHEST,
                         preferred_element_type=F32) + b_ref[...]


def _mods(cond, w_ada, b_ada):
    cols = 6 * D_MODEL
    blk = 1536
    return pl.pallas_call(
        _mods_kernel,
        grid=(cols // blk,),
        in_specs=[pl.BlockSpec((N_COND, D_MODEL), lambda j: (0, 0)),
                  pl.BlockSpec((D_MODEL, blk), lambda j: (0, j)),
                  pl.BlockSpec((1, blk), lambda j: (0, j))],
        out_specs=pl.BlockSpec((N_COND, blk), lambda j: (0, j)),
        out_shape=jax.ShapeDtypeStruct((N_COND, cols), F32),
        compiler_params=_cparams(("arbitrary",)),
        name="mods",
    )(cond, w_ada, b_ada.reshape(1, cols))


def _ctx_rows(width):
    return pl.BlockSpec((ROW_TILE, width), lambda i: (jnp.minimum(i, CTX_TILES - 1), 0))


def _lat_rows(width):
    return pl.BlockSpec((ROW_TILE, width), lambda i: (jnp.maximum(i - CTX_TILES, 0), 0))


def _tile_of(ctx_ref, lat_ref):
    return jnp.where(pl.program_id(0) < CTX_TILES, ctx_ref[...], lat_ref[...])


def _inproj_kernel(xc_ref, xl_ref, mod_ref, w_ref, h_ref, bg_ref, q_ref, k_ref, v_ref, gc_ref, gn_ref):
    sh1 = mod_ref[0, :, 0:D_MODEL]
    sc1 = mod_ref[0, :, D_MODEL:2 * D_MODEL]
    u = (_tile_of(xc_ref, xl_ref) * (1 + sc1) + sh1).astype(BF16)

    def proj(lo, hi):
        return jnp.dot(u, w_ref[:, lo:hi], preferred_element_type=F32)

    c = CONV_WIDTH
    a = NA_WIDTH
    xc = proj(0, c)
    cg = proj(2 * c, 3 * c)
    h_ref[...] = cg * xc
    bg_ref[...] = proj(c, 2 * c)
    o = 3 * c
    q_ref[...] = (proj(o, o + a) * (NA_HEAD_DIM ** -0.5)).astype(BF16)
    k_ref[...] = proj(o + a, o + 2 * a)
    v_ref[...] = proj(o + 2 * a, o + 3 * a)
    o = o + 3 * a
    gc_ref[...] = jax.nn.sigmoid(proj(o, o + D_MODEL))
    gn_ref[...] = jax.nn.sigmoid(proj(o + D_MODEL, o + 2 * D_MODEL))


def _inproj(x_ctx, x_lat, mods3, w_in_bf):
    row = lambda w: pl.BlockSpec((ROW_TILE, w), lambda i: (i, 0))
    return pl.pallas_call(
        _inproj_kernel,
        grid=(N_TILES,),
        in_specs=[_ctx_rows(D_MODEL), _lat_rows(D_MODEL),
                  pl.BlockSpec((1, 1, 6 * D_MODEL), lambda i: (_tile_mod_row(i), 0, 0)),
                  pl.BlockSpec((D_MODEL, IN_COLS), lambda i: (0, 0))],
        out_specs=[row(CONV_WIDTH), row(CONV_WIDTH), row(NA_WIDTH), row(NA_WIDTH), row(NA_WIDTH),
                   row(D_MODEL), row(D_MODEL)],
        out_shape=[jax.ShapeDtypeStruct((N_TOK, CONV_WIDTH), F32),
                   jax.ShapeDtypeStruct((N_TOK, CONV_WIDTH), F32),
                   jax.ShapeDtypeStruct((N_TOK, NA_WIDTH), BF16),
                   jax.ShapeDtypeStruct((N_TOK, NA_WIDTH), F32),
                   jax.ShapeDtypeStruct((N_TOK, NA_WIDTH), F32),
                   jax.ShapeDtypeStruct((N_TOK, D_MODEL), F32),
                   jax.ShapeDtypeStruct((N_TOK, D_MODEL), F32)],
        compiler_params=_cparams(("parallel",)),
        name="inproj",
    )(x_ctx, x_lat, mods3, w_in_bf)


def _head_pair_attention(q2, kv_list, bias_list):
    lane = lax.broadcasted_iota(jnp.int32, q2.shape, 1)
    outs = []
    for hh in range(2):
        in_head = (lane >= hh * NA_HEAD_DIM) & (lane < (hh + 1) * NA_HEAD_DIM)
        qh = jnp.where(in_head, q2, jnp.zeros_like(q2))
        scores = []
        for (k2, _), bias in zip(kv_list, bias_list):
            s = lax.dot_general(qh, k2, (((1,), (1,)), ((), ())), preferred_element_type=F32)
            if bias is not None:
                s = s + bias(hh)
            scores.append(s)
        m = scores[0].max(axis=-1, keepdims=True)
        for s in scores[1:]:
            m = jnp.maximum(m, s.max(axis=-1, keepdims=True))
        acc = None
        den = None
        for s, (_, v2) in zip(scores, kv_list):
            p = jnp.exp(s - m)
            d = p.sum(axis=-1, keepdims=True)
            o = jnp.dot(p.astype(BF16), v2, preferred_element_type=F32)
            acc = o if acc is None else acc + o
            den = d if den is None else den + d
        outs.append(acc / den)
    return jnp.where(lane < NA_HEAD_DIM, outs[0], outs[1])


def _ctx_attn_kernel(q_ref, k_ref, v_ref, o_ref):
    k2 = k_ref[...].astype(BF16)
    v2 = v_ref[...].astype(BF16)
    o_ref[...] = _head_pair_attention(q_ref[...], [(k2, v2)], [None]).astype(BF16)


def _ctx_attn(q, k, v):
    spec = pl.BlockSpec((SEQ, 128), lambda b, p: (b, p))
    return pl.pallas_call(
        _ctx_attn_kernel,
        grid=(BATCH, NA_WIDTH // 128),
        in_specs=[spec, spec, spec],
        out_specs=spec,
        out_shape=jax.ShapeDtypeStruct((N_CTX, NA_WIDTH), BF16),
        compiler_params=_cparams(("parallel", "parallel")),
        name="ctx_attn",
    )(q, k, v)


def _lat_attn_kernel(q_ref, k_ref, v_ref, kc_ref, vc_ref, cols_ref, o_ref, bias_ref):
    @pl.when(pl.program_id(2) == 0)
    def _build_bias():
        kh = min(WIN_H, GRID_ROWS)
        for lr in range(LAT_QB // GRID_W):
            qr = pl.program_id(0) * (LAT_QB // GRID_W) + lr
            row_start = jnp.clip(qr - kh // 2, 0, GRID_ROWS - kh)
            for kr in range(GRID_ROWS):
                in_rows = (kr >= row_start) & (kr < row_start + kh)
                dr = jnp.clip(kr - qr + (WIN_H - 1), 0, 2 * WIN_H - 2)
                for hh in range(2):
                    tile = jnp.where(in_rows, cols_ref[hh, dr], NEG_INF)
                    bias_ref[hh, lr * GRID_W:(lr + 1) * GRID_W, kr * GRID_W:(kr + 1) * GRID_W] = tile

    kw = k_ref[...].astype(BF16)
    vw = v_ref[...].astype(BF16)
    kc = kc_ref[0].astype(BF16)
    vc = vc_ref[0].astype(BF16)
    o = _head_pair_attention(q_ref[...], [(kw, vw), (kc, vc)], [lambda hh: bias_ref[hh], None])
    o_ref[...] = o.astype(BF16)


def _lat_attn(q, k, v, k_ctx, v_ctx, cols):
    n_qb = DEC_SEQ // LAT_QB
    q_off = N_CTX // LAT_QB
    kv_off = N_CTX // DEC_SEQ
    return pl.pallas_call(
        _lat_attn_kernel,
        grid=(n_qb, NA_WIDTH // 128, DEC_BATCH),
        in_specs=[pl.BlockSpec((LAT_QB, 128), lambda qb, p, b: (q_off + b * n_qb + qb, p)),
                  pl.BlockSpec((DEC_SEQ, 128), lambda qb, p, b: (kv_off + b, p)),
                  pl.BlockSpec((DEC_SEQ, 128), lambda qb, p, b: (kv_off + b, p)),
                  pl.BlockSpec((1, PAST_LEN, 128), lambda qb, p, b: (b, 0, p)),
                  pl.BlockSpec((1, PAST_LEN, 128), lambda qb, p, b: (b, 0, p)),
                  pl.BlockSpec((2, 2 * WIN_H - 1, GRID_W, GRID_W), lambda qb, p, b: (p, 0, 0, 0))],
        out_specs=pl.BlockSpec((LAT_QB, 128), lambda qb, p, b: (b * n_qb + qb, p)),
        out_shape=jax.ShapeDtypeStruct((N_LAT, NA_WIDTH), BF16),
        scratch_shapes=[pltpu.VMEM((2, LAT_QB, DEC_SEQ), F32)],
        compiler_params=_cparams(("parallel", "parallel", "arbitrary")),
        name="lat_attn",
    )(q, k, v, k_ctx, v_ctx, cols)


def _window_cols(rpb):
    col = np.arange(GRID_W)
    col_start = np.clip(col - WIN_W // 2, 0, GRID_W - WIN_W)
    col_in = (col[None, :] >= col_start[:, None]) & (col[None, :] < col_start[:, None] + WIN_W)
    dc = np.clip(col[None, :] - col[:, None], -(WIN_W - 1), WIN_W - 1) + (WIN_W - 1)
    onehot = (np.arange(2 * WIN_W - 1)[:, None, None] == dc[None]).astype(np.float32)
    cols = jnp.einsum('hrd,dqk->hrqk', rpb.astype(F32), jnp.asarray(onehot), precision=lax.Precision.HIGHEST)
    return jnp.where(jnp.asarray(col_in)[None, None], cols, NEG_INF)


def _layer_norm(x, g, b):
    mu = jnp.mean(x, axis=-1, keepdims=True)
    xc = x - mu
    var = jnp.mean(xc * xc, axis=-1, keepdims=True)
    return xc * lax.rsqrt(var + LN_EPS) * g + b


def _merge_kernel(xc_ref, xl_ref, mod_ref, h_ref, hp_ref, hn_ref, bg_ref, ync_ref, ynl_ref, gc_ref, gn_ref,
                  cw_ref, wco_ref, wno_ref, wo_ref, g1_ref, b1_ref, wqt_ref, sk_ref,
                  x1_ref, u2_ref, st_ref):
    i = pl.program_id(0)
    pos = (i - CTX_TILES) % LAT_TILES_PER_SEQ
    is_ctx = i < CTX_TILES
    has_prev = jnp.where(is_ctx | (pos == 0), 0.0, 1.0).astype(F32)
    has_next = jnp.where(is_ctx | (pos == LAT_TILES_PER_SEQ - 1), 0.0, 1.0).astype(F32)

    h = h_ref[...]
    rows = lax.broadcasted_iota(jnp.int32, h.shape, 0)
    prev_row = hp_ref[7:8, :] * has_prev
    next_row = hn_ref[0:1, :] * has_next
    h_prev = jnp.where(rows == 0, prev_row, pltpu.roll(h, 1, 0))
    h_next = jnp.where(rows == ROW_TILE - 1, next_row, pltpu.roll(h, ROW_TILE - 1, 0))
    y = cw_ref[0:1, :] * h_prev
    y = y + cw_ref[1:2, :] * h
    y = y + cw_ref[2:3, :] * h_next
    y_conv = (bg_ref[...] * y).astype(BF16)

    merged = (gc_ref[...] * jnp.dot(y_conv, wco_ref[...], preferred_element_type=F32) +
              gn_ref[...] * jnp.dot(_tile_of(ync_ref, ynl_ref), wno_ref[...],
                                    preferred_element_type=F32))
    z = jnp.dot(merged.astype(BF16), wo_ref[...], preferred_element_type=F32)

    ga1 = mod_ref[0, :, 2 * D_MODEL:3 * D_MODEL]
    sh2 = mod_ref[0, :, 3 * D_MODEL:4 * D_MODEL]
    sc2 = mod_ref[0, :, 4 * D_MODEL:5 * D_MODEL]
    x1 = _layer_norm(ALPHA * _tile_of(xc_ref, xl_ref) + ga1 * z, g1_ref[...], b1_ref[...])
    x1_ref[...] = x1
    u2 = (x1 * (1 + sc2) + sh2).astype(BF16)
    u2_ref[...] = u2

    qt = lax.dot_general(wqt_ref[...], u2, (((1,), (1,)), ((), ())), preferred_element_type=F32)
    half = PEER_KEY_DIM // 2
    for hp in range(2 * PEER_HEADS):
        blk = qt[hp * half:(hp + 1) * half, :].astype(BF16)
        st_ref[hp * N_KEYS:(hp + 1) * N_KEYS, :] = jnp.dot(sk_ref[hp], blk, preferred_element_type=F32)


def _merge(x_ctx, x_lat, mods3, h, bg, y_ctx, y_lat, gc, gn, conv_w, wco, wno, wo, g1, b1, wqt, sk):
    row = lambda w: pl.BlockSpec((ROW_TILE, w), lambda i: (i, 0))
    full = lambda *s: pl.BlockSpec(s, lambda i: (0,) * len(s))
    eights = ROW_TILE // 8
    n8 = N_TOK // 8
    return pl.pallas_call(
        _merge_kernel,
        grid=(N_TILES,),
        in_specs=[_ctx_rows(D_MODEL), _lat_rows(D_MODEL),
                  pl.BlockSpec((1, 1, 6 * D_MODEL), lambda i: (_tile_mod_row(i), 0, 0)),
                  row(CONV_WIDTH),
                  pl.BlockSpec((8, CONV_WIDTH), lambda i: (jnp.maximum(i * eights - 1, 0), 0)),
                  pl.BlockSpec((8, CONV_WIDTH), lambda i: (jnp.minimum((i + 1) * eights, n8 - 1), 0)),
                  row(CONV_WIDTH), _ctx_rows(NA_WIDTH), _lat_rows(NA_WIDTH), row(D_MODEL), row(D_MODEL),
                  full(CONV_K, CONV_WIDTH), full(CONV_WIDTH, D_MODEL), full(NA_WIDTH, D_MODEL),
                  full(D_MODEL, D_MODEL), full(1, D_MODEL), full(1, D_MODEL),
                  full(PEER_HEADS * PEER_KEY_DIM, D_MODEL),
                  full(2 * PEER_HEADS, N_KEYS, PEER_KEY_DIM // 2)],
        out_specs=[row(D_MODEL), row(D_MODEL),
                   pl.BlockSpec((2 * PEER_HEADS * N_KEYS, ROW_TILE), lambda i: (0, i))],
        out_shape=[jax.ShapeDtypeStruct((N_TOK, D_MODEL), F32),
                   jax.ShapeDtypeStruct((N_TOK, D_MODEL), BF16),
                   jax.ShapeDtypeStruct((2 * PEER_HEADS * N_KEYS, N_TOK), F32)],
        compiler_params=_cparams(("parallel",)),
        name="merge",
    )(x_ctx, x_lat, mods3, h, h, h, bg, y_ctx, y_lat, gc, gn, conv_w, wco, wno, wo, g1, b1, wqt, sk)


def _sort16_desc(v):
    v = list(v)
    n = len(v)
    k = 2
    while k <= n:
        j = k // 2
        while j >= 1:
            for i in range(n):
                m = i ^ j
                if m > i:
                    hi, lo = jnp.maximum(v[i], v[m]), jnp.minimum(v[i], v[m])
                    v[i], v[m] = (hi, lo) if (i & k) == 0 else (lo, hi)
            j //= 2
        k *= 2
    return v


def _merge_top16(x, y):
    n = len(x)
    keep = [jnp.maximum(x[i], y[n - 1 - i]) for i in range(n)]
    drop = [jnp.minimum(x[i], y[n - 1 - i]) for i in range(n)]
    while len(drop) > 1:
        drop = [jnp.maximum(drop[2 * i], drop[2 * i + 1]) for i in range(len(drop) // 2)]
    j = n // 2
    while j >= 1:
        for i in range(n):
            m = i ^ j
            if m > i:
                keep[i], keep[m] = jnp.maximum(keep[i], keep[m]), jnp.minimum(keep[i], keep[m])
        j //= 2
    return keep, drop[0]


def _select_kernel(st_ref, tau_ref, a_ref, m1_ref, zinv_ref, top_ref):
    n_chunks = N_KEYS // 8

    def one_list(hp, carry):
        base = pl.multiple_of(hp * N_KEYS, N_KEYS)
        cur = _sort16_desc([st_ref[pl.ds(base + 8 * c, 8), :] for c in range(n_chunks)])
        rest = None
        for shift in (4, 2, 1):
            cur, dropped = _merge_top16(cur, [pltpu.roll(x, shift, 0) for x in cur])
            if rest is not None:
                dropped = jnp.maximum(dropped, jnp.maximum(rest, pltpu.roll(rest, shift, 0)))
            rest = dropped
        for r in range(PEER_TOPK):
            top_ref[hp, r] = cur[r]
        top_ref[hp, PEER_TOPK] = rest
        return carry

    lax.fori_loop(0, 2 * PEER_HEADS, one_list, 0)

    sub = lax.broadcasted_iota(jnp.int32, (PEER_HEADS, SEL_TOK), 0)

    def packed(p, r):
        out = top_ref[p, r]
        for hd in range(1, PEER_HEADS):
            out = jnp.where(sub == hd, top_ref[2 * hd + p, r], out)
        return out

    a = [packed(0, r) for r in range(N_RANK)]
    b = [packed(1, r) for r in range(N_RANK)]
    cand = [a[r0] + b[r1] for (r0, r1) in STAIR]
    cand = cand + [jnp.full_like(cand[0], -jnp.inf)] * (-len(cand) % PEER_TOPK)
    blocks = [_sort16_desc(cand[i:i + PEER_TOPK]) for i in range(0, len(cand), PEER_TOPK)]
    rests = []
    while len(blocks) > 1:
        merged = []
        for i in range(0, len(blocks) - 1, 2):
            keep, dropped = _merge_top16(blocks[i], blocks[i + 1])
            merged.append(keep)
            rests.append(dropped)
        blocks = merged + blocks[len(blocks) - len(blocks) % 2:]
    top = blocks[0]
    c17 = rests[0]
    for x in rests[1:]:
        c17 = jnp.maximum(c17, x)
    z = jnp.ones_like(top[0])
    for x in top[1:]:
        z = z + jnp.exp(x - top[0])
    tau_ref[...] = 0.5 * (top[PEER_TOPK - 1] + c17)
    for r in range(PEER_TOPK):
        a_ref[r] = a[r]
    m1_ref[...] = b[0]
    zinv_ref[...] = 1.0 / z


def _select(st):
    n_tok = st.shape[1]
    stat = pl.BlockSpec((PEER_HEADS, SEL_TOK), lambda i: (0, i))
    shp = jax.ShapeDtypeStruct((PEER_HEADS, n_tok), F32)
    return pl.pallas_call(
        _select_kernel,
        grid=(n_tok // SEL_TOK,),
        in_specs=[pl.BlockSpec((2 * PEER_HEADS * N_KEYS, SEL_TOK), lambda i: (0, i))],
        out_specs=[stat, pl.BlockSpec((PEER_TOPK, PEER_HEADS, SEL_TOK), lambda i: (0, 0, i)), stat, stat],
        out_shape=[shp, jax.ShapeDtypeStruct((PEER_TOPK, PEER_HEADS, n_tok), F32), shp, shp],
        scratch_shapes=[pltpu.VMEM((2 * PEER_HEADS, N_RANK, 8, SEL_TOK), F32)],
        compiler_params=_cparams(("parallel",)),
        name="select",
    )(st)


def _gelu(x):
    return 0.5 * x * (1.0 + lax.erf(x * np.float32(np.sqrt(0.5))))


def _peer_gate_row(rows, key, s_ref, p_ref, rank0_ref, e0_ref, adm1_ref, e1_ref):
    for lane0 in range(0, PEER_TOK, PEER_LANE):
        lanes = slice(lane0, lane0 + PEER_LANE)
        w = jnp.zeros((N_KEYS, PEER_LANE), BF16)
        for hd in range(PEER_HEADS):
            adm1 = adm1_ref[hd * N_KEYS:(hd + 1) * N_KEYS, lanes]
            e1 = e1_ref[hd * N_KEYS:(hd + 1) * N_KEYS, lanes]
            rank0 = rank0_ref[pl.ds(hd * N_KEYS + key, 1), lanes].astype(BF16)
            e0 = e0_ref[pl.ds(hd * N_KEYS + key, 1), lanes].astype(BF16)
            w = w + jnp.where(rank0 < adm1, e1, jnp.zeros_like(e1)) * e0
        p_ref[rows, lanes] = w * _gelu(s_ref[rows, lanes]).astype(BF16)


def _peer_kernel(u2_ref, st_ref, tau_ref, a_ref, m1_ref, zinv_ref,
                 u_b_ref, u_a_next_ref, vt_b_prev_ref, vt_a_ref, vt_last_ref,
                 x1_ref, mod_ref, g2_ref, b2_ref, o_ref,
                 rank0_ref, e0_ref, adm1_ref, e1_ref, sa_ref, sb_ref, pa_ref, pb_ref, acc_ref):
    e = pl.program_id(1)
    rows_per_hb = PEER_HB // N_KEYS

    def pre_activation(u_ref):
        return lax.dot_general(u_ref[...], u2_ref[...], (((1,), (1,)), ((), ())),
                               preferred_element_type=F32)

    def phase(s_cur, p_cur, key_base, u_next_ref, s_next, vt_prev_ref, p_prev, first=None):
        s_next[...] = pre_activation(u_next_ref)
        for r in range(rows_per_hb):
            _peer_gate_row(slice(r * N_KEYS, (r + 1) * N_KEYS), key_base + r, s_cur, p_cur,
                           rank0_ref, e0_ref, adm1_ref, e1_ref)
        if first is None:
            acc_ref[...] += jnp.dot(vt_prev_ref[...], p_prev[...], preferred_element_type=F32)
        else:
            p = p_prev[...]
            prev = jnp.dot(vt_prev_ref[...], jnp.where(first, jnp.zeros_like(p), p),
                           preferred_element_type=F32)
            acc_ref[...] = jnp.where(first, 0.0, acc_ref[...]) + prev

    @pl.when(e == 0)
    def _prologue():
        half = N_KEYS // 2
        for hd in range(PEER_HEADS):
            for lane0 in range(0, PEER_TOK, 128):
                lanes = slice(lane0, lane0 + 128)
                a = [a_ref[r, hd:hd + 1, lanes] for r in range(PEER_TOPK)]
                tau = tau_ref[hd:hd + 1, lanes]
                for k0 in (0, half):
                    s0 = st_ref[(2 * hd) * N_KEYS + k0:(2 * hd) * N_KEYS + k0 + half, lanes]
                    s1 = st_ref[(2 * hd + 1) * N_KEYS + k0:(2 * hd + 1) * N_KEYS + k0 + half, lanes]
                    need = tau - s1
                    rank0 = jnp.zeros_like(s0)
                    adm1 = jnp.zeros_like(s1)
                    for r in range(PEER_TOPK):
                        rank0 = jnp.where(a[r] > s0, float(r + 1), rank0)
                        adm1 = jnp.where(a[r] >= need, float(r + 1), adm1)
                    rows = slice(hd * N_KEYS + k0, hd * N_KEYS + k0 + half)
                    rank0_ref[rows, lanes] = rank0
                    adm1_ref[rows, lanes] = adm1.astype(BF16)
                    e0_ref[rows, lanes] = jnp.exp(s0 - a[0]) * zinv_ref[hd:hd + 1, lanes]
                    e1_ref[rows, lanes] = jnp.exp(s1 - m1_ref[hd:hd + 1, lanes]).astype(BF16)
        acc_ref[...] = jnp.zeros_like(acc_ref)
        pb_ref[...] = jnp.zeros_like(pb_ref)
        sa_ref[...] = pre_activation(u_a_next_ref)

    @pl.when(e > 0)
    def _phases():
        a = 2 * (e - 1)
        phase(sa_ref, pa_ref, a * rows_per_hb, u_b_ref, sb_ref, vt_b_prev_ref, pb_ref, first=(e == 1))
        phase(sb_ref, pb_ref, (a + 1) * rows_per_hb, u_a_next_ref, sa_ref, vt_a_ref, pa_ref)

    @pl.when(e == pl.num_programs(1) - 1)
    def _epilogue():
        acc = acc_ref[...] + jnp.dot(vt_last_ref[...], pb_ref[...], preferred_element_type=F32)
        ga2 = mod_ref[0, :, 5 * D_MODEL:6 * D_MODEL]
        o_ref[...] = _layer_norm(ALPHA * x1_ref[...] + ga2 * acc.T, g2_ref[...], b2_ref[...])


def _peer(u2, st, tau, a_sorted, m1, zinv, u_bf, vt_bf, x1, mods3, g2, b2, latent):
    n_tok = N_LAT if latent else N_CTX
    t0 = (N_CTX if latent else 0) // PEER_TOK
    tiles_per_seq = DEC_SEQ // PEER_TOK
    mod_row = (lambda t: 1 + t // tiles_per_seq) if latent else (lambda t: 0)
    stat = pl.BlockSpec((PEER_HEADS, PEER_TOK), lambda t, e: (0, t0 + t))
    n_hb = N_EXPERTS // PEER_HB
    u_spec = lambda idx: pl.BlockSpec((PEER_HB, D_MODEL), lambda t, e: (idx(e), 0))
    vt_spec = lambda idx: pl.BlockSpec((D_MODEL, PEER_HB), lambda t, e: (0, idx(e)))
    return pl.pallas_call(
        _peer_kernel,
        grid=(n_tok // PEER_TOK, n_hb // 2 + 1),
        in_specs=[pl.BlockSpec((PEER_TOK, D_MODEL), lambda t, e: (t0 + t, 0)),
                  pl.BlockSpec((2 * PEER_HEADS * N_KEYS, PEER_TOK), lambda t, e: (0, t0 + t)),
                  stat,
                  pl.BlockSpec((PEER_TOPK, PEER_HEADS, PEER_TOK), lambda t, e: (0, 0, t0 + t)),
                  stat, stat,
                  u_spec(lambda e: jnp.maximum(2 * e - 1, 0)),
                  u_spec(lambda e: jnp.minimum(2 * e, n_hb - 1)),
                  vt_spec(lambda e: jnp.maximum(2 * e - 3, 0)),
                  vt_spec(lambda e: jnp.maximum(2 * e - 2, 0)),
                  vt_spec(lambda e: n_hb - 1),
                  pl.BlockSpec((PEER_TOK, D_MODEL), lambda t, e: (t0 + t, 0)),
                  pl.BlockSpec((1, 1, 6 * D_MODEL), lambda t, e: (mod_row(t), 0, 0)),
                  pl.BlockSpec((1, D_MODEL), lambda t, e: (0, 0)),
                  pl.BlockSpec((1, D_MODEL), lambda t, e: (0, 0))],
        out_specs=pl.BlockSpec((PEER_TOK, D_MODEL), lambda t, e: (t, 0)),
        out_shape=jax.ShapeDtypeStruct((n_tok, D_MODEL), F32),
        scratch_shapes=[pltpu.VMEM((PEER_HEADS * N_KEYS, PEER_TOK), F32),
                        pltpu.VMEM((PEER_HEADS * N_KEYS, PEER_TOK), F32),
                        pltpu.VMEM((PEER_HEADS * N_KEYS, PEER_TOK), BF16),
                        pltpu.VMEM((PEER_HEADS * N_KEYS, PEER_TOK), BF16),
                        pltpu.VMEM((PEER_HB, PEER_TOK), F32),
                        pltpu.VMEM((PEER_HB, PEER_TOK), F32),
                        pltpu.VMEM((PEER_HB, PEER_TOK), BF16),
                        pltpu.VMEM((PEER_HB, PEER_TOK), BF16),
                        pltpu.VMEM((D_MODEL, PEER_TOK), F32)],
        compiler_params=_cparams(("parallel", "arbitrary")),
        name="peer",
    )(u2, st, tau, a_sorted, m1, zinv, u_bf, u_bf, vt_bf, vt_bf, vt_bf, x1, mods3, g2, b2)


def kernel(x_prompt, x_sample, cache_k, cache_v, c, c_ctx, w_ada, b_ada, w_in, conv_w, w_conv_out,
           w_na_out, rpb, w_o, ln1_g, ln1_b, w_query, sub_keys, expert_u, expert_v, ln2_g, ln2_b):
    l = 0
    x_ctx = x_prompt.reshape(N_CTX, D_MODEL)
    x_lat = x_sample.reshape(N_LAT, D_MODEL)
    cond = jnp.concatenate([c_ctx[None, :], c, jnp.zeros((N_COND - 1 - DEC_BATCH, D_MODEL), F32)], axis=0)

    mods = _mods(cond, w_ada[l], b_ada[l])
    mods3 = mods.reshape(N_COND, 1, 6 * D_MODEL)

    h, bg, q, k, v, gc, gn = _inproj(x_ctx, x_lat, mods3, w_in[l].astype(BF16))

    y_ctx = _ctx_attn(q, k, v)
    k_ctx = cache_k[:, l].reshape(DEC_BATCH, PAST_LEN, NA_WIDTH)
    v_ctx = cache_v[:, l].reshape(DEC_BATCH, PAST_LEN, NA_WIDTH)
    y_lat = _lat_attn(q, k, v, k_ctx, v_ctx, _window_cols(rpb[l]))

    wqt = w_query[l].T.astype(BF16)
    sk = sub_keys[l].reshape(2 * PEER_HEADS, N_KEYS, PEER_KEY_DIM // 2).astype(BF16)
    x1, u2, st = _merge(x_ctx, x_lat, mods3, h, bg, y_ctx, y_lat, gc, gn, conv_w[l],
                        w_conv_out[l].astype(BF16), w_na_out[l].astype(BF16), w_o[l].astype(BF16),
                        ln1_g[l].reshape(1, D_MODEL), ln1_b[l].reshape(1, D_MODEL), wqt, sk)

    tau, a_sorted, m1, zinv = _select(st)
    peer_args = (u2, st, tau, a_sorted, m1, zinv, expert_u[l].astype(BF16), expert_v[l].T.astype(BF16),
                 x1, mods3, ln2_g[l].reshape(1, D_MODEL), ln2_b[l].reshape(1, D_MODEL))
    y_prompt = _peer(*peer_args, latent=False).reshape(BATCH, SEQ, D_MODEL)
    y_sample = _peer(*peer_args, latent=True).reshape(DEC_BATCH, DEC_SEQ, D_MODEL)
    new_k = k[:N_CTX].reshape(BATCH, 1, SEQ, NA_HEADS, NA_HEAD_DIM)
    new_v = v[:N_CTX].reshape(BATCH, 1, SEQ, NA_HEADS, NA_HEAD_DIM)
    return (y_prompt, y_sample, new_k, new_v)
```

```python
import functools

import numpy as np
import jax
import jax.numpy as jnp
from jax import lax
from jax.experimental import pallas as pl
from jax.experimental.pallas import tpu as pltpu

F32 = jnp.float32
BF16 = jnp.bfloat16

D_MODEL = 1024
BATCH = 16
SEQ = 256
DEC_BATCH = 8
DEC_SEQ = 1024
PAST_LEN = 512
GRID_W = 64
GRID_ROWS = DEC_SEQ // GRID_W
CONV_WIDTH = 512
CONV_K = 3
NA_HEADS = 8
NA_HEAD_DIM = 64
NA_WIDTH = NA_HEADS * NA_HEAD_DIM
WIN_H = 8
WIN_W = 16
PEER_HEADS = 8
PEER_KEY_DIM = 256
N_KEYS = 128
N_EXPERTS = N_KEYS * N_KEYS
PEER_TOPK = 16
LN_EPS = 1e-5
NEG_INF = -1e30
DEPTH = 1
ALPHA = (2 * DEPTH) ** 0.25
IN_COLS = 3 * CONV_WIDTH + 3 * NA_WIDTH + 2 * D_MODEL

N_CTX = BATCH * SEQ
N_LAT = DEC_BATCH * DEC_SEQ
N_TOK = N_CTX + N_LAT
N_COND = 16

ROW_TILE = 256
CTX_TILES = N_CTX // ROW_TILE
LAT_TILES_PER_SEQ = DEC_SEQ // ROW_TILE
N_TILES = N_TOK // ROW_TILE

PEER_TOK = 512
PEER_LANE = 256
PEER_EB = 2048
PEER_ROWS_PER_EB = PEER_EB // N_KEYS
SEL_TOK = 256
LAT_QB = 256

VMEM_LIMIT = 56 * 1024 * 1024

N_RANK = PEER_TOPK + 1
STAIR = [(a, b) for a in range(N_RANK) for b in range(N_RANK) if (a + 1) * (b + 1) <= N_RANK]


def _cparams(sem):
    return pltpu.CompilerParams(dimension_semantics=sem, vmem_limit_bytes=VMEM_LIMIT)


def _tile_mod_row(i):
    return jnp.where(i < CTX_TILES, 0, 1 + (i - CTX_TILES) // LAT_TILES_PER_SEQ)


def _mods_kernel(cond_ref, w_ref, b_ref, o_ref):
    cnd = cond_ref[...]
    act = cnd * jax.nn.sigmoid(cnd)
    o_ref[...] = jnp.dot(act, w_ref[...], precision=lax.Precision.HIGHEST,
                         preferred_element_type=F32) + b_ref[...]


def _mods(cond, w_ada, b_ada):
    cols = 6 * D_MODEL
    blk = 1536
    return pl.pallas_call(
        _mods_kernel,
        grid=(cols // blk,),
        in_specs=[pl.BlockSpec((N_COND, D_MODEL), lambda j: (0, 0)),
                  pl.BlockSpec((D_MODEL, blk), lambda j: (0, j)),
                  pl.BlockSpec((1, blk), lambda j: (0, j))],
        out_specs=pl.BlockSpec((N_COND, blk), lambda j: (0, j)),
        out_shape=jax.ShapeDtypeStruct((N_COND, cols), F32),
        compiler_params=_cparams(("arbitrary",)),
        name="mods",
    )(cond, w_ada, b_ada.reshape(1, cols))


def _ctx_rows(width):
    return pl.BlockSpec((ROW_TILE, width), lambda i: (jnp.minimum(i, CTX_TILES - 1), 0))


def _lat_rows(width):
    return pl.BlockSpec((ROW_TILE, width), lambda i: (jnp.maximum(i - CTX_TILES, 0), 0))


def _tile_of(ctx_ref, lat_ref):
    return jnp.where(pl.program_id(0) < CTX_TILES, ctx_ref[...], lat_ref[...])


def _inproj_kernel(xc_ref, xl_ref, mod_ref, w_ref, h_ref, bg_ref, q_ref, k_ref, v_ref, gc_ref, gn_ref):
    sh1 = mod_ref[0, :, 0:D_MODEL]
    sc1 = mod_ref[0, :, D_MODEL:2 * D_MODEL]
    u = (_tile_of(xc_ref, xl_ref) * (1 + sc1) + sh1).astype(BF16)

    def proj(lo, hi):
        return jnp.dot(u, w_ref[:, lo:hi], preferred_element_type=F32)

    c = CONV_WIDTH
    a = NA_WIDTH
    xc = proj(0, c)
    cg = proj(2 * c, 3 * c)
    h_ref[...] = cg * xc
    bg_ref[...] = proj(c, 2 * c)
    o = 3 * c
    q_ref[...] = (proj(o, o + a) * (NA_HEAD_DIM ** -0.5)).astype(BF16)
    k_ref[...] = proj(o + a, o + 2 * a)
    v_ref[...] = proj(o + 2 * a, o + 3 * a)
    o = o + 3 * a
    gc_ref[...] = jax.nn.sigmoid(proj(o, o + D_MODEL))
    gn_ref[...] = jax.nn.sigmoid(proj(o + D_MODEL, o + 2 * D_MODEL))


def _inproj(x_ctx, x_lat, mods3, w_in_bf):
    row = lambda w: pl.BlockSpec((ROW_TILE, w), lambda i: (i, 0))
    return pl.pallas_call(
        _inproj_kernel,
        grid=(N_TILES,),
        in_specs=[_ctx_rows(D_MODEL), _lat_rows(D_MODEL),
                  pl.BlockSpec((1, 1, 6 * D_MODEL), lambda i: (_tile_mod_row(i), 0, 0)),
                  pl.BlockSpec((D_MODEL, IN_COLS), lambda i: (0, 0))],
        out_specs=[row(CONV_WIDTH), row(CONV_WIDTH), row(NA_WIDTH), row(NA_WIDTH), row(NA_WIDTH),
                   row(D_MODEL), row(D_MODEL)],
        out_shape=[jax.ShapeDtypeStruct((N_TOK, CONV_WIDTH), F32),
                   jax.ShapeDtypeStruct((N_TOK, CONV_WIDTH), F32),
                   jax.ShapeDtypeStruct((N_TOK, NA_WIDTH), BF16),
                   jax.ShapeDtypeStruct((N_TOK, NA_WIDTH), F32),
                   jax.ShapeDtypeStruct((N_TOK, NA_WIDTH), F32),
                   jax.ShapeDtypeStruct((N_TOK, D_MODEL), F32),
                   jax.ShapeDtypeStruct((N_TOK, D_MODEL), F32)],
        compiler_params=_cparams(("parallel",)),
        name="inproj",
    )(x_ctx, x_lat, mods3, w_in_bf)


def _head_pair_attention(q2, kv_list, bias_list):
    lane = lax.broadcasted_iota(jnp.int32, q2.shape, 1)
    outs = []
    for hh in range(2):
        in_head = (lane >= hh * NA_HEAD_DIM) & (lane < (hh + 1) * NA_HEAD_DIM)
        qh = jnp.where(in_head, q2, jnp.zeros_like(q2))
        scores = []
        for (k2, _), bias in zip(kv_list, bias_list):
            s = lax.dot_general(qh, k2, (((1,), (1,)), ((), ())), preferred_element_type=F32)
            if bias is not None:
                s = s + bias(hh)
            scores.append(s)
        m = scores[0].max(axis=-1, keepdims=True)
        for s in scores[1:]:
            m = jnp.maximum(m, s.max(axis=-1, keepdims=True))
        acc = None
        den = None
        for s, (_, v2) in zip(scores, kv_list):
            p = jnp.exp(s - m)
            d = p.sum(axis=-1, keepdims=True)
            o = jnp.dot(p.astype(BF16), v2, preferred_element_type=F32)
            acc = o if acc is None else acc + o
            den = d if den is None else den + d
        outs.append(acc / den)
    return jnp.where(lane < NA_HEAD_DIM, outs[0], outs[1])


def _ctx_attn_kernel(q_ref, k_ref, v_ref, o_ref):
    for p in range(NA_WIDTH // 128):
        lanes = slice(p * 128, (p + 1) * 128)
        k2 = k_ref[:, lanes].astype(BF16)
        v2 = v_ref[:, lanes].astype(BF16)
        o_ref[:, lanes] = _head_pair_attention(q_ref[:, lanes], [(k2, v2)], [None]).astype(BF16)


def _ctx_attn(q, k, v):
    spec = pl.BlockSpec((SEQ, NA_WIDTH), lambda b: (b, 0))
    return pl.pallas_call(
        _ctx_attn_kernel,
        grid=(BATCH,),
        in_specs=[spec, spec, spec],
        out_specs=spec,
        out_shape=jax.ShapeDtypeStruct((N_CTX, NA_WIDTH), BF16),
        compiler_params=_cparams(("parallel",)),
        name="ctx_attn",
    )(q, k, v)


def _lat_attn_kernel(q_ref, k_ref, v_ref, kc_ref, vc_ref, cols_ref, o_ref, bias_ref):
    n_pairs = NA_WIDTH // 128

    @pl.when(pl.program_id(1) == 0)
    def _build_bias():
        kh = min(WIN_H, GRID_ROWS)
        for lr in range(LAT_QB // GRID_W):
            qr = pl.program_id(0) * (LAT_QB // GRID_W) + lr
            row_start = jnp.clip(qr - kh // 2, 0, GRID_ROWS - kh)
            for kr in range(GRID_ROWS):
                in_rows = (kr >= row_start) & (kr < row_start + kh)
                dr = jnp.clip(kr - qr + (WIN_H - 1), 0, 2 * WIN_H - 2)
                for hd in range(NA_HEADS):
                    tile = jnp.where(in_rows, cols_ref[hd, dr], NEG_INF)
                    bias_ref[hd, lr * GRID_W:(lr + 1) * GRID_W, kr * GRID_W:(kr + 1) * GRID_W] = tile

    for p in range(n_pairs):
        lanes = slice(p * 128, (p + 1) * 128)
        kw = k_ref[:, lanes].astype(BF16)
        vw = v_ref[:, lanes].astype(BF16)
        kc = kc_ref[0, :, lanes].astype(BF16)
        vc = vc_ref[0, :, lanes].astype(BF16)
        o = _head_pair_attention(q_ref[:, lanes], [(kw, vw), (kc, vc)],
                                 [lambda hh, p=p: bias_ref[2 * p + hh], None])
        o_ref[:, lanes] = o.astype(BF16)


def _lat_attn(q, k, v, k_ctx, v_ctx, cols):
    n_qb = DEC_SEQ // LAT_QB
    q_off = N_CTX // LAT_QB
    kv_off = N_CTX // DEC_SEQ
    return pl.pallas_call(
        _lat_attn_kernel,
        grid=(n_qb, DEC_BATCH),
        in_specs=[pl.BlockSpec((LAT_QB, NA_WIDTH), lambda qb, b: (q_off + b * n_qb + qb, 0)),
                  pl.BlockSpec((DEC_SEQ, NA_WIDTH), lambda qb, b: (kv_off + b, 0)),
                  pl.BlockSpec((DEC_SEQ, NA_WIDTH), lambda qb, b: (kv_off + b, 0)),
                  pl.BlockSpec((1, PAST_LEN, NA_WIDTH), lambda qb, b: (b, 0, 0)),
                  pl.BlockSpec((1, PAST_LEN, NA_WIDTH), lambda qb, b: (b, 0, 0)),
                  pl.BlockSpec((NA_HEADS, 2 * WIN_H - 1, GRID_W, GRID_W), lambda qb, b: (0, 0, 0, 0))],
        out_specs=pl.BlockSpec((LAT_QB, NA_WIDTH), lambda qb, b: (b * n_qb + qb, 0)),
        out_shape=jax.ShapeDtypeStruct((N_LAT, NA_WIDTH), BF16),
        scratch_shapes=[pltpu.VMEM((NA_HEADS, LAT_QB, DEC_SEQ), F32)],
        compiler_params=_cparams(("parallel", "arbitrary")),
        name="lat_attn",
    )(q, k, v, k_ctx, v_ctx, cols)


def _window_cols(rpb):
    col = np.arange(GRID_W)
    col_start = np.clip(col - WIN_W // 2, 0, GRID_W - WIN_W)
    col_in = (col[None, :] >= col_start[:, None]) & (col[None, :] < col_start[:, None] + WIN_W)
    dc = np.clip(col[None, :] - col[:, None], -(WIN_W - 1), WIN_W - 1) + (WIN_W - 1)
    onehot = (np.arange(2 * WIN_W - 1)[:, None, None] == dc[None]).astype(np.float32)
    cols = jnp.einsum('hrd,dqk->hrqk', rpb.astype(F32), jnp.asarray(onehot), precision=lax.Precision.HIGHEST)
    return jnp.where(jnp.asarray(col_in)[None, None], cols, NEG_INF)


def _layer_norm(x, g, b):
    mu = jnp.mean(x, axis=-1, keepdims=True)
    xc = x - mu
    var = jnp.mean(xc * xc, axis=-1, keepdims=True)
    return xc * lax.rsqrt(var + LN_EPS) * g + b


def _merge_kernel(xc_ref, xl_ref, mod_ref, h_ref, hp_ref, hn_ref, bg_ref, ync_ref, ynl_ref, gc_ref, gn_ref,
                  cw_ref, wco_ref, wno_ref, wo_ref, g1_ref, b1_ref, wqt_ref, sk_ref,
                  x1_ref, u2_ref, st_ref):
    i = pl.program_id(0)
    pos = (i - CTX_TILES) % LAT_TILES_PER_SEQ
    is_ctx = i < CTX_TILES
    has_prev = jnp.where(is_ctx | (pos == 0), 0.0, 1.0).astype(F32)
    has_next = jnp.where(is_ctx | (pos == LAT_TILES_PER_SEQ - 1), 0.0, 1.0).astype(F32)

    h = h_ref[...]
    rows = lax.broadcasted_iota(jnp.int32, h.shape, 0)
    prev_row = hp_ref[7:8, :] * has_prev
    next_row = hn_ref[0:1, :] * has_next
    h_prev = jnp.where(rows == 0, prev_row, pltpu.roll(h, 1, 0))
    h_next = jnp.where(rows == ROW_TILE - 1, next_row, pltpu.roll(h, ROW_TILE - 1, 0))
    y = cw_ref[0:1, :] * h_prev
    y = y + cw_ref[1:2, :] * h
    y = y + cw_ref[2:3, :] * h_next
    y_conv = (bg_ref[...] * y).astype(BF16)

    merged = (gc_ref[...] * jnp.dot(y_conv, wco_ref[...], preferred_element_type=F32) +
              gn_ref[...] * jnp.dot(_tile_of(ync_ref, ynl_ref), wno_ref[...],
                                    preferred_element_type=F32))
    z = jnp.dot(merged.astype(BF16), wo_ref[...], preferred_element_type=F32)

    ga1 = mod_ref[0, :, 2 * D_MODEL:3 * D_MODEL]
    sh2 = mod_ref[0, :, 3 * D_MODEL:4 * D_MODEL]
    sc2 = mod_ref[0, :, 4 * D_MODEL:5 * D_MODEL]
    x1 = _layer_norm(ALPHA * _tile_of(xc_ref, xl_ref) + ga1 * z, g1_ref[...], b1_ref[...])
    x1_ref[...] = x1
    u2 = (x1 * (1 + sc2) + sh2).astype(BF16)
    u2_ref[...] = u2

    qt = lax.dot_general(wqt_ref[...], u2, (((1,), (1,)), ((), ())), preferred_element_type=F32)
    half = PEER_KEY_DIM // 2
    for hp in range(2 * PEER_HEADS):
        blk = qt[hp * half:(hp + 1) * half, :].astype(BF16)
        st_ref[hp * N_KEYS:(hp + 1) * N_KEYS, :] = jnp.dot(sk_ref[hp], blk, preferred_element_type=F32)


def _merge(x_ctx, x_lat, mods3, h, bg, y_ctx, y_lat, gc, gn, conv_w, wco, wno, wo, g1, b1, wqt, sk):
    row = lambda w: pl.BlockSpec((ROW_TILE, w), lambda i: (i, 0))
    full = lambda *s: pl.BlockSpec(s, lambda i: (0,) * len(s))
    eights = ROW_TILE // 8
    n8 = N_TOK // 8
    return pl.pallas_call(
        _merge_kernel,
        grid=(N_TILES,),
        in_specs=[_ctx_rows(D_MODEL), _lat_rows(D_MODEL),
                  pl.BlockSpec((1, 1, 6 * D_MODEL), lambda i: (_tile_mod_row(i), 0, 0)),
                  row(CONV_WIDTH),
                  pl.BlockSpec((8, CONV_WIDTH), lambda i: (jnp.maximum(i * eights - 1, 0), 0)),
                  pl.BlockSpec((8, CONV_WIDTH), lambda i: (jnp.minimum((i + 1) * eights, n8 - 1), 0)),
                  row(CONV_WIDTH), _ctx_rows(NA_WIDTH), _lat_rows(NA_WIDTH), row(D_MODEL), row(D_MODEL),
                  full(CONV_K, CONV_WIDTH), full(CONV_WIDTH, D_MODEL), full(NA_WIDTH, D_MODEL),
                  full(D_MODEL, D_MODEL), full(1, D_MODEL), full(1, D_MODEL),
                  full(PEER_HEADS * PEER_KEY_DIM, D_MODEL),
                  full(2 * PEER_HEADS, N_KEYS, PEER_KEY_DIM // 2)],
        out_specs=[row(D_MODEL), row(D_MODEL),
                   pl.BlockSpec((2 * PEER_HEADS * N_KEYS, ROW_TILE), lambda i: (0, i))],
        out_shape=[jax.ShapeDtypeStruct((N_TOK, D_MODEL), F32),
                   jax.ShapeDtypeStruct((N_TOK, D_MODEL), BF16),
                   jax.ShapeDtypeStruct((2 * PEER_HEADS * N_KEYS, N_TOK), F32)],
        compiler_params=_cparams(("parallel",)),
        name="merge",
    )(x_ctx, x_lat, mods3, h, h, h, bg, y_ctx, y_lat, gc, gn, conv_w, wco, wno, wo, g1, b1, wqt, sk)


def _sort16_desc(v):
    v = list(v)
    n = len(v)
    k = 2
    while k <= n:
        j = k // 2
        while j >= 1:
            for i in range(n):
                m = i ^ j
                if m > i:
                    hi, lo = jnp.maximum(v[i], v[m]), jnp.minimum(v[i], v[m])
                    v[i], v[m] = (hi, lo) if (i & k) == 0 else (lo, hi)
            j //= 2
        k *= 2
    return v


def _merge_top16(x, y):
    n = len(x)
    keep = [jnp.maximum(x[i], y[n - 1 - i]) for i in range(n)]
    drop = [jnp.minimum(x[i], y[n - 1 - i]) for i in range(n)]
    while len(drop) > 1:
        drop = [jnp.maximum(drop[2 * i], drop[2 * i + 1]) for i in range(len(drop) // 2)]
    j = n // 2
    while j >= 1:
        for i in range(n):
            m = i ^ j
            if m > i:
                keep[i], keep[m] = jnp.maximum(keep[i], keep[m]), jnp.minimum(keep[i], keep[m])
        j //= 2
    return keep, drop[0]


def _select_kernel(st_ref, tau_ref, a_ref, m1_ref, zinv_ref, top_ref):
    n_chunks = N_KEYS // 8

    def one_list(hp, carry):
        base = pl.multiple_of(hp * N_KEYS, N_KEYS)
        cur = _sort16_desc([st_ref[pl.ds(base + 8 * c, 8), :] for c in range(n_chunks)])
        rest = None
        for shift in (4, 2, 1):
            cur, dropped = _merge_top16(cur, [pltpu.roll(x, shift, 0) for x in cur])
            if rest is not None:
                dropped = jnp.maximum(dropped, jnp.maximum(rest, pltpu.roll(rest, shift, 0)))
            rest = dropped
        for r in range(PEER_TOPK):
            top_ref[hp, r] = cur[r]
        top_ref[hp, PEER_TOPK] = rest
        return carry

    lax.fori_loop(0, 2 * PEER_HEADS, one_list, 0)

    sub = lax.broadcasted_iota(jnp.int32, (PEER_HEADS, SEL_TOK), 0)

    def packed(p, r):
        out = top_ref[p, r]
        for hd in range(1, PEER_HEADS):
            out = jnp.where(sub == hd, top_ref[2 * hd + p, r], out)
        return out

    a = [packed(0, r) for r in range(N_RANK)]
    b = [packed(1, r) for r in range(N_RANK)]
    cand = [a[r0] + b[r1] for (r0, r1) in STAIR]
    cand = cand + [jnp.full_like(cand[0], -jnp.inf)] * (-len(cand) % PEER_TOPK)
    blocks = [_sort16_desc(cand[i:i + PEER_TOPK]) for i in range(0, len(cand), PEER_TOPK)]
    rests = []
    while len(blocks) > 1:
        merged = []
        for i in range(0, len(blocks) - 1, 2):
            keep, dropped = _merge_top16(blocks[i], blocks[i + 1])
            merged.append(keep)
            rests.append(dropped)
        blocks = merged + blocks[len(blocks) - len(blocks) % 2:]
    top = blocks[0]
    c17 = rests[0]
    for x in rests[1:]:
        c17 = jnp.maximum(c17, x)
    z = jnp.ones_like(top[0])
    for x in top[1:]:
        z = z + jnp.exp(x - top[0])
    tau_ref[...] = 0.5 * (top[PEER_TOPK - 1] + c17)
    for r in range(PEER_TOPK):
        a_ref[r] = a[r]
    m1_ref[...] = b[0]
    zinv_ref[...] = 1.0 / z


def _select(st):
    n_tok = st.shape[1]
    stat = pl.BlockSpec((PEER_HEADS, SEL_TOK), lambda i: (0, i))
    shp = jax.ShapeDtypeStruct((PEER_HEADS, n_tok), F32)
    return pl.pallas_call(
        _select_kernel,
        grid=(n_tok // SEL_TOK,),
        in_specs=[pl.BlockSpec((2 * PEER_HEADS * N_KEYS, SEL_TOK), lambda i: (0, i))],
        out_specs=[stat, pl.BlockSpec((PEER_TOPK, PEER_HEADS, SEL_TOK), lambda i: (0, 0, i)), stat, stat],
        out_shape=[shp, jax.ShapeDtypeStruct((PEER_TOPK, PEER_HEADS, n_tok), F32), shp, shp],
        scratch_shapes=[pltpu.VMEM((2 * PEER_HEADS, N_RANK, 8, SEL_TOK), F32)],
        compiler_params=_cparams(("parallel",)),
        name="select",
    )(st)


def _gelu(x):
    return 0.5 * x * (1.0 + lax.erf(x * np.float32(np.sqrt(0.5))))


def _peer_kernel(u2_ref, st_ref, tau_ref, a_ref, m1_ref, zinv_ref, u_ref, vt_ref,
                 x1_ref, mod_ref, g2_ref, b2_ref, o_ref,
                 rank0_ref, e0_ref, adm1_ref, e1_ref, s_ref, p_ref, acc_ref):
    eb = pl.program_id(1)

    @pl.when(eb == 0)
    def _prologue():
        half = N_KEYS // 2
        for hd in range(PEER_HEADS):
            for lane0 in range(0, PEER_TOK, 128):
                lanes = slice(lane0, lane0 + 128)
                a = [a_ref[r, hd:hd + 1, lanes] for r in range(PEER_TOPK)]
                tau = tau_ref[hd:hd + 1, lanes]
                for k0 in (0, half):
                    s0 = st_ref[(2 * hd) * N_KEYS + k0:(2 * hd) * N_KEYS + k0 + half, lanes]
                    s1 = st_ref[(2 * hd + 1) * N_KEYS + k0:(2 * hd + 1) * N_KEYS + k0 + half, lanes]
                    need = tau - s1
                    rank0 = jnp.zeros_like(s0)
                    adm1 = jnp.zeros_like(s1)
                    for r in range(PEER_TOPK):
                        rank0 = jnp.where(a[r] > s0, float(r + 1), rank0)
                        adm1 = jnp.where(a[r] >= need, float(r + 1), adm1)
                    rows = slice(hd * N_KEYS + k0, hd * N_KEYS + k0 + half)
                    rank0_ref[rows, lanes] = rank0
                    adm1_ref[rows, lanes] = adm1.astype(BF16)
                    e0_ref[rows, lanes] = jnp.exp(s0 - a[0]) * zinv_ref[hd:hd + 1, lanes]
                    e1_ref[rows, lanes] = jnp.exp(s1 - m1_ref[hd:hd + 1, lanes]).astype(BF16)
        acc_ref[...] = jnp.zeros_like(acc_ref)

    s_ref[...] = lax.dot_general(u_ref[...], u2_ref[...], (((1,), (1,)), ((), ())),
                                 preferred_element_type=F32)

    def gate_rows(r, carry):
        rows = pl.ds(pl.multiple_of(r * N_KEYS, N_KEYS), N_KEYS)
        key0 = eb * PEER_ROWS_PER_EB + r
        for lane0 in range(0, PEER_TOK, PEER_LANE):
            lanes = slice(lane0, lane0 + PEER_LANE)
            w = jnp.zeros((N_KEYS, PEER_LANE), BF16)
            for hd in range(PEER_HEADS):
                adm1 = adm1_ref[hd * N_KEYS:(hd + 1) * N_KEYS, lanes]
                e1 = e1_ref[hd * N_KEYS:(hd + 1) * N_KEYS, lanes]
                rank0 = rank0_ref[pl.ds(hd * N_KEYS + key0, 1), lanes].astype(BF16)
                e0 = e0_ref[pl.ds(hd * N_KEYS + key0, 1), lanes].astype(BF16)
                w = w + jnp.where(rank0 < adm1, e1, jnp.zeros_like(e1)) * e0
            p_ref[rows, lanes] = w * _gelu(s_ref[rows, lanes]).astype(BF16)
        return carry

    lax.fori_loop(0, PEER_ROWS_PER_EB, gate_rows, 0)

    acc_ref[...] += jnp.dot(vt_ref[...], p_ref[...], preferred_element_type=F32)

    @pl.when(eb == pl.num_programs(1) - 1)
    def _epilogue():
        y = acc_ref[...].T
        ga2 = mod_ref[0, :, 5 * D_MODEL:6 * D_MODEL]
        o_ref[...] = _layer_norm(ALPHA * x1_ref[...] + ga2 * y, g2_ref[...], b2_ref[...])


def _peer(u2, st, tau, a_sorted, m1, zinv, u_bf, vt_bf, x1, mods3, g2, b2, latent):
    n_tok = N_LAT if latent else N_CTX
    t0 = (N_CTX if latent else 0) // PEER_TOK
    tiles_per_seq = DEC_SEQ // PEER_TOK
    mod_row = (lambda t: 1 + t // tiles_per_seq) if latent else (lambda t: 0)
    stat = pl.BlockSpec((PEER_HEADS, PEER_TOK), lambda t, e: (0, t0 + t))
    return pl.pallas_call(
        _peer_kernel,
        grid=(n_tok // PEER_TOK, N_EXPERTS // PEER_EB),
        in_specs=[pl.BlockSpec((PEER_TOK, D_MODEL), lambda t, e: (t0 + t, 0)),
                  pl.BlockSpec((2 * PEER_HEADS * N_KEYS, PEER_TOK), lambda t, e: (0, t0 + t)),
                  stat,
                  pl.BlockSpec((PEER_TOPK, PEER_HEADS, PEER_TOK), lambda t, e: (0, 0, t0 + t)),
                  stat, stat,
                  pl.BlockSpec((PEER_EB, D_MODEL), lambda t, e: (e, 0)),
                  pl.BlockSpec((D_MODEL, PEER_EB), lambda t, e: (0, e)),
                  pl.BlockSpec((PEER_TOK, D_MODEL), lambda t, e: (t0 + t, 0)),
                  pl.BlockSpec((1, 1, 6 * D_MODEL), lambda t, e: (mod_row(t), 0, 0)),
                  pl.BlockSpec((1, D_MODEL), lambda t, e: (0, 0)),
                  pl.BlockSpec((1, D_MODEL), lambda t, e: (0, 0))],
        out_specs=pl.BlockSpec((PEER_TOK, D_MODEL), lambda t, e: (t, 0)),
        out_shape=jax.ShapeDtypeStruct((n_tok, D_MODEL), F32),
        scratch_shapes=[pltpu.VMEM((PEER_HEADS * N_KEYS, PEER_TOK), F32),
                        pltpu.VMEM((PEER_HEADS * N_KEYS, PEER_TOK), F32),
                        pltpu.VMEM((PEER_HEADS * N_KEYS, PEER_TOK), BF16),
                        pltpu.VMEM((PEER_HEADS * N_KEYS, PEER_TOK), BF16),
                        pltpu.VMEM((PEER_EB, PEER_TOK), F32),
                        pltpu.VMEM((PEER_EB, PEER_TOK), BF16),
                        pltpu.VMEM((D_MODEL, PEER_TOK), F32)],
        compiler_params=_cparams(("parallel", "arbitrary")),
        name="peer",
    )(u2, st, tau, a_sorted, m1, zinv, u_bf, vt_bf, x1, mods3, g2, b2)


def kernel(x_prompt, x_sample, cache_k, cache_v, c, c_ctx, w_ada, b_ada, w_in, conv_w, w_conv_out,
           w_na_out, rpb, w_o, ln1_g, ln1_b, w_query, sub_keys, expert_u, expert_v, ln2_g, ln2_b):
    l = 0
    x_ctx = x_prompt.reshape(N_CTX, D_MODEL)
    x_lat = x_sample.reshape(N_LAT, D_MODEL)
    cond = jnp.concatenate([c_ctx[None, :], c, jnp.zeros((N_COND - 1 - DEC_BATCH, D_MODEL), F32)], axis=0)

    mods = _mods(cond, w_ada[l], b_ada[l])
    mods3 = mods.reshape(N_COND, 1, 6 * D_MODEL)

    h, bg, q, k, v, gc, gn = _inproj(x_ctx, x_lat, mods3, w_in[l].astype(BF16))

    y_ctx = _ctx_attn(q, k, v)
    k_ctx = cache_k[:, l].reshape(DEC_BATCH, PAST_LEN, NA_WIDTH)
    v_ctx = cache_v[:, l].reshape(DEC_BATCH, PAST_LEN, NA_WIDTH)
    y_lat = _lat_attn(q, k, v, k_ctx, v_ctx, _window_cols(rpb[l]))

    wqt = w_query[l].T.astype(BF16)
    sk = sub_keys[l].reshape(2 * PEER_HEADS, N_KEYS, PEER_KEY_DIM // 2).astype(BF16)
    x1, u2, st = _merge(x_ctx, x_lat, mods3, h, bg, y_ctx, y_lat, gc, gn, conv_w[l],
                        w_conv_out[l].astype(BF16), w_na_out[l].astype(BF16), w_o[l].astype(BF16),
                        ln1_g[l].reshape(1, D_MODEL), ln1_b[l].reshape(1, D_MODEL), wqt, sk)

    tau, a_sorted, m1, zinv = _select(st)
    peer_args = (u2, st, tau, a_sorted, m1, zinv, expert_u[l].astype(BF16), expert_v[l].T.astype(BF16),
                 x1, mods3, ln2_g[l].reshape(1, D_MODEL), ln2_b[l].reshape(1, D_MODEL))
    y_prompt = _peer(*peer_args, latent=False).reshape(BATCH, SEQ, D_MODEL)
    y_sample = _peer(*peer_args, latent=True).reshape(DEC_BATCH, DEC_SEQ, D_MODEL)
    new_k = k[:N_CTX].reshape(BATCH, 1, SEQ, NA_HEADS, NA_HEAD_DIM)
    new_v = v[:N_CTX].reshape(BATCH, 1, SEQ, NA_HEADS, NA_HEAD_DIM)
    return (y_prompt, y_sample, new_k, new_v)
```

```python
import functools

import numpy as np
import jax
import jax.numpy as jnp
from jax import lax
from jax.experimental import pallas as pl
from jax.experimental.pallas import tpu as pltpu

F32 = jnp.float32
BF16 = jnp.bfloat16

D_MODEL = 1024
BATCH = 16
SEQ = 256
DEC_BATCH = 8
DEC_SEQ = 1024
PAST_LEN = 512
GRID_W = 64
GRID_ROWS = DEC_SEQ // GRID_W
CONV_WIDTH = 512
CONV_K = 3
NA_HEADS = 8
NA_HEAD_DIM = 64
NA_WIDTH = NA_HEADS * NA_HEAD_DIM
WIN_H = 8
WIN_W = 16
PEER_HEADS = 8
PEER_KEY_DIM = 256
N_KEYS = 128
N_EXPERTS = N_KEYS * N_KEYS
PEER_TOPK = 16
LN_EPS = 1e-5
NEG_INF = -1e30
DEPTH = 1
ALPHA = (2 * DEPTH) ** 0.25
IN_COLS = 3 * CONV_WIDTH + 3 * NA_WIDTH + 2 * D_MODEL

N_CTX = BATCH * SEQ
N_LAT = DEC_BATCH * DEC_SEQ
N_TOK = N_CTX + N_LAT
N_COND = 16

ROW_TILE = 256
CTX_TILES = N_CTX // ROW_TILE
LAT_TILES_PER_SEQ = DEC_SEQ // ROW_TILE
N_TILES = N_TOK // ROW_TILE

PEER_TOK = 512
PEER_LANE = 256
PEER_EB = 2048
PEER_ROWS_PER_EB = PEER_EB // N_KEYS
SEL_TOK = 256
LAT_QB = 256

VMEM_LIMIT = 56 * 1024 * 1024

N_RANK = PEER_TOPK + 1
STAIR = [(a, b) for a in range(N_RANK) for b in range(N_RANK) if (a + 1) * (b + 1) <= N_RANK]


def _cparams(sem):
    return pltpu.CompilerParams(dimension_semantics=sem, vmem_limit_bytes=VMEM_LIMIT)


def _tile_mod_row(i):
    return jnp.where(i < CTX_TILES, 0, 1 + (i - CTX_TILES) // LAT_TILES_PER_SEQ)


def _mods_kernel(cond_ref, w_ref, b_ref, o_ref):
    cnd = cond_ref[...]
    act = cnd * jax.nn.sigmoid(cnd)
    o_ref[...] = jnp.dot(act, w_ref[...], precision=lax.Precision.HIGHEST,
                         preferred_element_type=F32) + b_ref[...]


def _mods(cond, w_ada, b_ada):
    cols = 6 * D_MODEL
    blk = 1536
    return pl.pallas_call(
        _mods_kernel,
        grid=(cols // blk,),
        in_specs=[pl.BlockSpec((N_COND, D_MODEL), lambda j: (0, 0)),
                  pl.BlockSpec((D_MODEL, blk), lambda j: (0, j)),
                  pl.BlockSpec((1, blk), lambda j: (0, j))],
        out_specs=pl.BlockSpec((N_COND, blk), lambda j: (0, j)),
        out_shape=jax.ShapeDtypeStruct((N_COND, cols), F32),
        compiler_params=_cparams(("arbitrary",)),
        name="mods",
    )(cond, w_ada, b_ada.reshape(1, cols))


def _ctx_rows(width):
    return pl.BlockSpec((ROW_TILE, width), lambda i: (jnp.minimum(i, CTX_TILES - 1), 0))


def _lat_rows(width):
    return pl.BlockSpec((ROW_TILE, width), lambda i: (jnp.maximum(i - CTX_TILES, 0), 0))


def _tile_of(ctx_ref, lat_ref):
    return jnp.where(pl.program_id(0) < CTX_TILES, ctx_ref[...], lat_ref[...])


def _inproj_kernel(xc_ref, xl_ref, mod_ref, w_ref, h_ref, bg_ref, q_ref, k_ref, v_ref, gc_ref, gn_ref):
    sh1 = mod_ref[0, :, 0:D_MODEL]
    sc1 = mod_ref[0, :, D_MODEL:2 * D_MODEL]
    u = (_tile_of(xc_ref, xl_ref) * (1 + sc1) + sh1).astype(BF16)

    def proj(lo, hi):
        return jnp.dot(u, w_ref[:, lo:hi], preferred_element_type=F32)

    c = CONV_WIDTH
    a = NA_WIDTH
    xc = proj(0, c)
    cg = proj(2 * c, 3 * c)
    h_ref[...] = cg * xc
    bg_ref[...] = proj(c, 2 * c)
    o = 3 * c
    q_ref[...] = (proj(o, o + a) * (NA_HEAD_DIM ** -0.5)).astype(BF16)
    k_ref[...] = proj(o + a, o + 2 * a)
    v_ref[...] = proj(o + 2 * a, o + 3 * a)
    o = o + 3 * a
    gc_ref[...] = jax.nn.sigmoid(proj(o, o + D_MODEL))
    gn_ref[...] = jax.nn.sigmoid(proj(o + D_MODEL, o + 2 * D_MODEL))


def _inproj(x_ctx, x_lat, mods3, w_in_bf):
    row = lambda w: pl.BlockSpec((ROW_TILE, w), lambda i: (i, 0))
    return pl.pallas_call(
        _inproj_kernel,
        grid=(N_TILES,),
        in_specs=[_ctx_rows(D_MODEL), _lat_rows(D_MODEL),
                  pl.BlockSpec((1, 1, 6 * D_MODEL), lambda i: (_tile_mod_row(i), 0, 0)),
                  pl.BlockSpec((D_MODEL, IN_COLS), lambda i: (0, 0))],
        out_specs=[row(CONV_WIDTH), row(CONV_WIDTH), row(NA_WIDTH), row(NA_WIDTH), row(NA_WIDTH),
                   row(D_MODEL), row(D_MODEL)],
        out_shape=[jax.ShapeDtypeStruct((N_TOK, CONV_WIDTH), F32),
                   jax.ShapeDtypeStruct((N_TOK, CONV_WIDTH), F32),
                   jax.ShapeDtypeStruct((N_TOK, NA_WIDTH), BF16),
                   jax.ShapeDtypeStruct((N_TOK, NA_WIDTH), F32),
                   jax.ShapeDtypeStruct((N_TOK, NA_WIDTH), F32),
                   jax.ShapeDtypeStruct((N_TOK, D_MODEL), F32),
                   jax.ShapeDtypeStruct((N_TOK, D_MODEL), F32)],
        compiler_params=_cparams(("parallel",)),
        name="inproj",
    )(x_ctx, x_lat, mods3, w_in_bf)


def _head_pair_attention(q2, kv_list, bias_list):
    lane = lax.broadcasted_iota(jnp.int32, q2.shape, 1)
    outs = []
    for hh in range(2):
        in_head = (lane >= hh * NA_HEAD_DIM) & (lane < (hh + 1) * NA_HEAD_DIM)
        qh = jnp.where(in_head, q2, jnp.zeros_like(q2))
        scores = []
        for (k2, _), bias in zip(kv_list, bias_list):
            s = lax.dot_general(qh, k2, (((1,), (1,)), ((), ())), preferred_element_type=F32)
            if bias is not None:
                s = s + bias(hh)
            scores.append(s)
        m = scores[0].max(axis=-1, keepdims=True)
        for s in scores[1:]:
            m = jnp.maximum(m, s.max(axis=-1, keepdims=True))
        acc = None
        den = None
        for s, (_, v2) in zip(scores, kv_list):
            p = jnp.exp(s - m)
            d = p.sum(axis=-1, keepdims=True)
            o = jnp.dot(p.astype(BF16), v2, preferred_element_type=F32)
            acc = o if acc is None else acc + o
            den = d if den is None else den + d
        outs.append(acc / den)
    return jnp.where(lane < NA_HEAD_DIM, outs[0], outs[1])


def _ctx_attn_kernel(q_ref, k_ref, v_ref, o_ref):
    for p in range(NA_WIDTH // 128):
        lanes = slice(p * 128, (p + 1) * 128)
        k2 = k_ref[:, lanes].astype(BF16)
        v2 = v_ref[:, lanes].astype(BF16)
        o_ref[:, lanes] = _head_pair_attention(q_ref[:, lanes], [(k2, v2)], [None]).astype(BF16)


def _ctx_attn(q, k, v):
    spec = pl.BlockSpec((SEQ, NA_WIDTH), lambda b: (b, 0))
    return pl.pallas_call(
        _ctx_attn_kernel,
        grid=(BATCH,),
        in_specs=[spec, spec, spec],
        out_specs=spec,
        out_shape=jax.ShapeDtypeStruct((N_CTX, NA_WIDTH), BF16),
        compiler_params=_cparams(("parallel",)),
        name="ctx_attn",
    )(q, k, v)


def _lat_attn_kernel(q_ref, k_ref, v_ref, kc_ref, vc_ref, cols_ref, o_ref, bias_ref):
    n_pairs = NA_WIDTH // 128

    @pl.when(pl.program_id(1) == 0)
    def _build_bias():
        kh = min(WIN_H, GRID_ROWS)
        for lr in range(LAT_QB // GRID_W):
            qr = pl.program_id(0) * (LAT_QB // GRID_W) + lr
            row_start = jnp.clip(qr - kh // 2, 0, GRID_ROWS - kh)
            for kr in range(GRID_ROWS):
                in_rows = (kr >= row_start) & (kr < row_start + kh)
                dr = jnp.clip(kr - qr + (WIN_H - 1), 0, 2 * WIN_H - 2)
                for hd in range(NA_HEADS):
                    tile = jnp.where(in_rows, cols_ref[hd, dr], NEG_INF)
                    bias_ref[hd, lr * GRID_W:(lr + 1) * GRID_W, kr * GRID_W:(kr + 1) * GRID_W] = tile

    for p in range(n_pairs):
        lanes = slice(p * 128, (p + 1) * 128)
        kw = k_ref[:, lanes].astype(BF16)
        vw = v_ref[:, lanes].astype(BF16)
        kc = kc_ref[0, :, lanes].astype(BF16)
        vc = vc_ref[0, :, lanes].astype(BF16)
        o = _head_pair_attention(q_ref[:, lanes], [(kw, vw), (kc, vc)],
                                 [lambda hh, p=p: bias_ref[2 * p + hh], None])
        o_ref[:, lanes] = o.astype(BF16)


def _lat_attn(q, k, v, k_ctx, v_ctx, cols):
    n_qb = DEC_SEQ // LAT_QB
    q_off = N_CTX // LAT_QB
    kv_off = N_CTX // DEC_SEQ
    return pl.pallas_call(
        _lat_attn_kernel,
        grid=(n_qb, DEC_BATCH),
        in_specs=[pl.BlockSpec((LAT_QB, NA_WIDTH), lambda qb, b: (q_off + b * n_qb + qb, 0)),
                  pl.BlockSpec((DEC_SEQ, NA_WIDTH), lambda qb, b: (kv_off + b, 0)),
                  pl.BlockSpec((DEC_SEQ, NA_WIDTH), lambda qb, b: (kv_off + b, 0)),
                  pl.BlockSpec((1, PAST_LEN, NA_WIDTH), lambda qb, b: (b, 0, 0)),
                  pl.BlockSpec((1, PAST_LEN, NA_WIDTH), lambda qb, b: (b, 0, 0)),
                  pl.BlockSpec((NA_HEADS, 2 * WIN_H - 1, GRID_W, GRID_W), lambda qb, b: (0, 0, 0, 0))],
        out_specs=pl.BlockSpec((LAT_QB, NA_WIDTH), lambda qb, b: (b * n_qb + qb, 0)),
        out_shape=jax.ShapeDtypeStruct((N_LAT, NA_WIDTH), BF16),
        scratch_shapes=[pltpu.VMEM((NA_HEADS, LAT_QB, DEC_SEQ), F32)],
        compiler_params=_cparams(("parallel", "arbitrary")),
        name="lat_attn",
    )(q, k, v, k_ctx, v_ctx, cols)


def _window_cols(rpb):
    col = np.arange(GRID_W)
    col_start = np.clip(col - WIN_W // 2, 0, GRID_W - WIN_W)
    col_in = (col[None, :] >= col_start[:, None]) & (col[None, :] < col_start[:, None] + WIN_W)
    dc = np.clip(col[None, :] - col[:, None], -(WIN_W - 1), WIN_W - 1) + (WIN_W - 1)
    onehot = (np.arange(2 * WIN_W - 1)[:, None, None] == dc[None]).astype(np.float32)
    cols = jnp.einsum('hrd,dqk->hrqk', rpb.astype(F32), jnp.asarray(onehot), precision=lax.Precision.HIGHEST)
    return jnp.where(jnp.asarray(col_in)[None, None], cols, NEG_INF)


def _layer_norm(x, g, b):
    mu = jnp.mean(x, axis=-1, keepdims=True)
    xc = x - mu
    var = jnp.mean(xc * xc, axis=-1, keepdims=True)
    return xc * lax.rsqrt(var + LN_EPS) * g + b


def _merge_kernel(xc_ref, xl_ref, mod_ref, h_ref, hp_ref, hn_ref, bg_ref, ync_ref, ynl_ref, gc_ref, gn_ref,
                  cw_ref, wco_ref, wno_ref, wo_ref, g1_ref, b1_ref, wqt_ref, sk_ref,
                  x1_ref, u2_ref, st_ref):
    i = pl.program_id(0)
    pos = (i - CTX_TILES) % LAT_TILES_PER_SEQ
    is_ctx = i < CTX_TILES
    has_prev = jnp.where(is_ctx | (pos == 0), 0.0, 1.0).astype(F32)
    has_next = jnp.where(is_ctx | (pos == LAT_TILES_PER_SEQ - 1), 0.0, 1.0).astype(F32)

    h = h_ref[...]
    rows = lax.broadcasted_iota(jnp.int32, h.shape, 0)
    prev_row = hp_ref[7:8, :] * has_prev
    next_row = hn_ref[0:1, :] * has_next
    h_prev = jnp.where(rows == 0, prev_row, pltpu.roll(h, 1, 0))
    h_next = jnp.where(rows == ROW_TILE - 1, next_row, pltpu.roll(h, ROW_TILE - 1, 0))
    y = cw_ref[0:1, :] * h_prev
    y = y + cw_ref[1:2, :] * h
    y = y + cw_ref[2:3, :] * h_next
    y_conv = (bg_ref[...] * y).astype(BF16)

    merged = (gc_ref[...] * jnp.dot(y_conv, wco_ref[...], preferred_element_type=F32) +
              gn_ref[...] * jnp.dot(_tile_of(ync_ref, ynl_ref), wno_ref[...],
                                    preferred_element_type=F32))
    z = jnp.dot(merged.astype(BF16), wo_ref[...], preferred_element_type=F32)

    ga1 = mod_ref[0, :, 2 * D_MODEL:3 * D_MODEL]
    sh2 = mod_ref[0, :, 3 * D_MODEL:4 * D_MODEL]
    sc2 = mod_ref[0, :, 4 * D_MODEL:5 * D_MODEL]
    x1 = _layer_norm(ALPHA * _tile_of(xc_ref, xl_ref) + ga1 * z, g1_ref[...], b1_ref[...])
    x1_ref[...] = x1
    u2 = (x1 * (1 + sc2) + sh2).astype(BF16)
    u2_ref[...] = u2

    qt = lax.dot_general(wqt_ref[...], u2, (((1,), (1,)), ((), ())), preferred_element_type=F32)
    half = PEER_KEY_DIM // 2
    for hp in range(2 * PEER_HEADS):
        blk = qt[hp * half:(hp + 1) * half, :].astype(BF16)
        st_ref[hp * N_KEYS:(hp + 1) * N_KEYS, :] = jnp.dot(sk_ref[hp], blk, preferred_element_type=F32)


def _merge(x_ctx, x_lat, mods3, h, bg, y_ctx, y_lat, gc, gn, conv_w, wco, wno, wo, g1, b1, wqt, sk):
    row = lambda w: pl.BlockSpec((ROW_TILE, w), lambda i: (i, 0))
    full = lambda *s: pl.BlockSpec(s, lambda i: (0,) * len(s))
    eights = ROW_TILE // 8
    n8 = N_TOK // 8
    return pl.pallas_call(
        _merge_kernel,
        grid=(N_TILES,),
        in_specs=[_ctx_rows(D_MODEL), _lat_rows(D_MODEL),
                  pl.BlockSpec((1, 1, 6 * D_MODEL), lambda i: (_tile_mod_row(i), 0, 0)),
                  row(CONV_WIDTH),
                  pl.BlockSpec((8, CONV_WIDTH), lambda i: (jnp.maximum(i * eights - 1, 0), 0)),
                  pl.BlockSpec((8, CONV_WIDTH), lambda i: (jnp.minimum((i + 1) * eights, n8 - 1), 0)),
                  row(CONV_WIDTH), _ctx_rows(NA_WIDTH), _lat_rows(NA_WIDTH), row(D_MODEL), row(D_MODEL),
                  full(CONV_K, CONV_WIDTH), full(CONV_WIDTH, D_MODEL), full(NA_WIDTH, D_MODEL),
                  full(D_MODEL, D_MODEL), full(1, D_MODEL), full(1, D_MODEL),
                  full(PEER_HEADS * PEER_KEY_DIM, D_MODEL),
                  full(2 * PEER_HEADS, N_KEYS, PEER_KEY_DIM // 2)],
        out_specs=[row(D_MODEL), row(D_MODEL),
                   pl.BlockSpec((2 * PEER_HEADS * N_KEYS, ROW_TILE), lambda i: (0, i))],
        out_shape=[jax.ShapeDtypeStruct((N_TOK, D_MODEL), F32),
                   jax.ShapeDtypeStruct((N_TOK, D_MODEL), BF16),
                   jax.ShapeDtypeStruct((2 * PEER_HEADS * N_KEYS, N_TOK), F32)],
        compiler_params=_cparams(("parallel",)),
        name="merge",
    )(x_ctx, x_lat, mods3, h, h, h, bg, y_ctx, y_lat, gc, gn, conv_w, wco, wno, wo, g1, b1, wqt, sk)


def _sort16_desc(v):
    v = list(v)
    n = len(v)
    k = 2
    while k <= n:
        j = k // 2
        while j >= 1:
            for i in range(n):
                m = i ^ j
                if m > i:
                    hi, lo = jnp.maximum(v[i], v[m]), jnp.minimum(v[i], v[m])
                    v[i], v[m] = (hi, lo) if (i & k) == 0 else (lo, hi)
            j //= 2
        k *= 2
    return v


def _merge_top16(x, y):
    n = len(x)
    keep = [jnp.maximum(x[i], y[n - 1 - i]) for i in range(n)]
    drop = [jnp.minimum(x[i], y[n - 1 - i]) for i in range(n)]
    while len(drop) > 1:
        drop = [jnp.maximum(drop[2 * i], drop[2 * i + 1]) for i in range(len(drop) // 2)]
    j = n // 2
    while j >= 1:
        for i in range(n):
            m = i ^ j
            if m > i:
                keep[i], keep[m] = jnp.maximum(keep[i], keep[m]), jnp.minimum(keep[i], keep[m])
        j //= 2
    return keep, drop[0]


def _select_kernel(st_ref, tau_ref, a_ref, m1_ref, zinv_ref, top_ref):
    n_chunks = N_KEYS // 8

    def one_list(hp, carry):
        base = pl.multiple_of(hp * N_KEYS, N_KEYS)
        cur = _sort16_desc([st_ref[pl.ds(base + 8 * c, 8), :] for c in range(n_chunks)])
        rest = None
        for shift in (4, 2, 1):
            cur, dropped = _merge_top16(cur, [pltpu.roll(x, shift, 0) for x in cur])
            if rest is not None:
                dropped = jnp.maximum(dropped, jnp.maximum(rest, pltpu.roll(rest, shift, 0)))
            rest = dropped
        for r in range(PEER_TOPK):
            top_ref[hp, r] = cur[r]
        top_ref[hp, PEER_TOPK] = rest
        return carry

    lax.fori_loop(0, 2 * PEER_HEADS, one_list, 0)

    sub = lax.broadcasted_iota(jnp.int32, (PEER_HEADS, SEL_TOK), 0)

    def packed(p, r):
        out = top_ref[p, r]
        for hd in range(1, PEER_HEADS):
            out = jnp.where(sub == hd, top_ref[2 * hd + p, r], out)
        return out

    a = [packed(0, r) for r in range(N_RANK)]
    b = [packed(1, r) for r in range(N_RANK)]
    cand = [a[r0] + b[r1] for (r0, r1) in STAIR]
    cand = cand + [jnp.full_like(cand[0], -jnp.inf)] * (-len(cand) % PEER_TOPK)
    blocks = [_sort16_desc(cand[i:i + PEER_TOPK]) for i in range(0, len(cand), PEER_TOPK)]
    rests = []
    while len(blocks) > 1:
        merged = []
        for i in range(0, len(blocks) - 1, 2):
            keep, dropped = _merge_top16(blocks[i], blocks[i + 1])
            merged.append(keep)
            rests.append(dropped)
        blocks = merged + blocks[len(blocks) - len(blocks) % 2:]
    top = blocks[0]
    c17 = rests[0]
    for x in rests[1:]:
        c17 = jnp.maximum(c17, x)
    z = jnp.ones_like(top[0])
    for x in top[1:]:
        z = z + jnp.exp(x - top[0])
    tau_ref[...] = 0.5 * (top[PEER_TOPK - 1] + c17)
    for r in range(PEER_TOPK):
        a_ref[r] = a[r]
    m1_ref[...] = b[0]
    zinv_ref[...] = 1.0 / z


def _select(st):
    n_tok = st.shape[1]
    stat = pl.BlockSpec((PEER_HEADS, SEL_TOK), lambda i: (0, i))
    shp = jax.ShapeDtypeStruct((PEER_HEADS, n_tok), F32)
    return pl.pallas_call(
        _select_kernel,
        grid=(n_tok // SEL_TOK,),
        in_specs=[pl.BlockSpec((2 * PEER_HEADS * N_KEYS, SEL_TOK), lambda i: (0, i))],
        out_specs=[stat, pl.BlockSpec((PEER_TOPK, PEER_HEADS, SEL_TOK), lambda i: (0, 0, i)), stat, stat],
        out_shape=[shp, jax.ShapeDtypeStruct((PEER_TOPK, PEER_HEADS, n_tok), F32), shp, shp],
        scratch_shapes=[pltpu.VMEM((2 * PEER_HEADS, N_RANK, 8, SEL_TOK), F32)],
        compiler_params=_cparams(("parallel",)),
        name="select",
    )(st)


def _row_bf16(row):
    return jnp.broadcast_to(row, (16, row.shape[1])).astype(BF16)


def _twice_gelu(x):
    return x + x * lax.erf(x * np.float32(np.sqrt(0.5)))


def _peer_kernel(u2_ref, st_ref, tau_ref, a_ref, m1_ref, zinv_ref, u_ref, vt_ref,
                 x1_ref, mod_ref, g2_ref, b2_ref, o_ref,
                 rank0_ref, e0_ref, adm1_ref, e1_ref, s_ref, p_ref, acc_ref):
    eb = pl.program_id(1)

    @pl.when(eb == 0)
    def _prologue():
        half = N_KEYS // 2
        for hd in range(PEER_HEADS):
            for lane0 in range(0, PEER_TOK, 128):
                lanes = slice(lane0, lane0 + 128)
                a = [a_ref[r, hd:hd + 1, lanes] for r in range(PEER_TOPK)]
                tau = tau_ref[hd:hd + 1, lanes]
                for k0 in (0, half):
                    s0 = st_ref[(2 * hd) * N_KEYS + k0:(2 * hd) * N_KEYS + k0 + half, lanes]
                    s1 = st_ref[(2 * hd + 1) * N_KEYS + k0:(2 * hd + 1) * N_KEYS + k0 + half, lanes]
                    need = tau - s1
                    rank0 = jnp.zeros_like(s0)
                    adm1 = jnp.zeros_like(s1)
                    for r in range(PEER_TOPK):
                        rank0 = jnp.where(a[r] > s0, float(r + 1), rank0)
                        adm1 = jnp.where(a[r] >= need, float(r + 1), adm1)
                    rows = slice(hd * N_KEYS + k0, hd * N_KEYS + k0 + half)
                    rank0_ref[rows, lanes] = rank0
                    adm1_ref[rows, lanes] = adm1.astype(BF16)
                    e0_ref[rows, lanes] = jnp.exp(s0 - a[0]) * (0.5 * zinv_ref[hd:hd + 1, lanes])
                    e1_ref[rows, lanes] = jnp.exp(s1 - m1_ref[hd:hd + 1, lanes]).astype(BF16)
        acc_ref[...] = jnp.zeros_like(acc_ref)

    s_ref[...] = lax.dot_general(u_ref[...], u2_ref[...], (((1,), (1,)), ((), ())),
                                 preferred_element_type=F32)

    def gate_rows(r, carry):
        rows = pl.ds(pl.multiple_of(r * N_KEYS, N_KEYS), N_KEYS)
        key0 = eb * PEER_ROWS_PER_EB + r
        tile = (N_KEYS // 16, 16, PEER_LANE)
        for lane0 in range(0, PEER_TOK, PEER_LANE):
            lanes = slice(lane0, lane0 + PEER_LANE)
            w = jnp.zeros(tile, BF16)
            for hd in range(PEER_HEADS):
                adm1 = adm1_ref[hd * N_KEYS:(hd + 1) * N_KEYS, lanes].reshape(tile)
                e1 = e1_ref[hd * N_KEYS:(hd + 1) * N_KEYS, lanes].reshape(tile)
                rank0 = _row_bf16(rank0_ref[pl.ds(hd * N_KEYS + key0, 1), lanes])[None]
                e0 = _row_bf16(e0_ref[pl.ds(hd * N_KEYS + key0, 1), lanes])[None]
                w = w + jnp.where(rank0 < adm1, e1, jnp.zeros_like(e1)) * e0
            g2 = _twice_gelu(s_ref[rows, lanes]).astype(BF16)
            p_ref[rows, lanes] = w.reshape(N_KEYS, PEER_LANE) * g2
        return carry

    lax.fori_loop(0, PEER_ROWS_PER_EB, gate_rows, 0)

    acc_ref[...] += jnp.dot(vt_ref[...], p_ref[...], preferred_element_type=F32)

    @pl.when(eb == pl.num_programs(1) - 1)
    def _epilogue():
        y = acc_ref[...].T
        ga2 = mod_ref[0, :, 5 * D_MODEL:6 * D_MODEL]
        o_ref[...] = _layer_norm(ALPHA * x1_ref[...] + ga2 * y, g2_ref[...], b2_ref[...])


def _peer(u2, st, tau, a_sorted, m1, zinv, u_bf, vt_bf, x1, mods3, g2, b2, latent):
    n_tok = N_LAT if latent else N_CTX
    t0 = (N_CTX if latent else 0) // PEER_TOK
    tiles_per_seq = DEC_SEQ // PEER_TOK
    mod_row = (lambda t: 1 + t // tiles_per_seq) if latent else (lambda t: 0)
    stat = pl.BlockSpec((PEER_HEADS, PEER_TOK), lambda t, e: (0, t0 + t))
    return pl.pallas_call(
        _peer_kernel,
        grid=(n_tok // PEER_TOK, N_EXPERTS // PEER_EB),
        in_specs=[pl.BlockSpec((PEER_TOK, D_MODEL), lambda t, e: (t0 + t, 0)),
                  pl.BlockSpec((2 * PEER_HEADS * N_KEYS, PEER_TOK), lambda t, e: (0, t0 + t)),
                  stat,
                  pl.BlockSpec((PEER_TOPK, PEER_HEADS, PEER_TOK), lambda t, e: (0, 0, t0 + t)),
                  stat, stat,
                  pl.BlockSpec((PEER_EB, D_MODEL), lambda t, e: (e, 0)),
                  pl.BlockSpec((D_MODEL, PEER_EB), lambda t, e: (0, e)),
                  pl.BlockSpec((PEER_TOK, D_MODEL), lambda t, e: (t0 + t, 0)),
                  pl.BlockSpec((1, 1, 6 * D_MODEL), lambda t, e: (mod_row(t), 0, 0)),
                  pl.BlockSpec((1, D_MODEL), lambda t, e: (0, 0)),
                  pl.BlockSpec((1, D_MODEL), lambda t, e: (0, 0))],
        out_specs=pl.BlockSpec((PEER_TOK, D_MODEL), lambda t, e: (t, 0)),
        out_shape=jax.ShapeDtypeStruct((n_tok, D_MODEL), F32),
        scratch_shapes=[pltpu.VMEM((PEER_HEADS * N_KEYS, PEER_TOK), F32),
                        pltpu.VMEM((PEER_HEADS * N_KEYS, PEER_TOK), F32),
                        pltpu.VMEM((PEER_HEADS * N_KEYS, PEER_TOK), BF16),
                        pltpu.VMEM((PEER_HEADS * N_KEYS, PEER_TOK), BF16),
                        pltpu.VMEM((PEER_EB, PEER_TOK), F32),
                        pltpu.VMEM((PEER_EB, PEER_TOK), BF16),
                        pltpu.VMEM((D_MODEL, PEER_TOK), F32)],
        compiler_params=_cparams(("parallel", "arbitrary")),
        name="peer",
    )(u2, st, tau, a_sorted, m1, zinv, u_bf, vt_bf, x1, mods3, g2, b2)


def kernel(x_prompt, x_sample, cache_k, cache_v, c, c_ctx, w_ada, b_ada, w_in, conv_w, w_conv_out,
           w_na_out, rpb, w_o, ln1_g, ln1_b, w_query, sub_keys, expert_u, expert_v, ln2_g, ln2_b):
    l = 0
    x_ctx = x_prompt.reshape(N_CTX, D_MODEL)
    x_lat = x_sample.reshape(N_LAT, D_MODEL)
    cond = jnp.concatenate([c_ctx[None, :], c, jnp.zeros((N_COND - 1 - DEC_BATCH, D_MODEL), F32)], axis=0)

    mods = _mods(cond, w_ada[l], b_ada[l])
    mods3 = mods.reshape(N_COND, 1, 6 * D_MODEL)

    h, bg, q, k, v, gc, gn = _inproj(x_ctx, x_lat, mods3, w_in[l].astype(BF16))

    y_ctx = _ctx_attn(q, k, v)
    k_ctx = cache_k[:, l].reshape(DEC_BATCH, PAST_LEN, NA_WIDTH)
    v_ctx = cache_v[:, l].reshape(DEC_BATCH, PAST_LEN, NA_WIDTH)
    y_lat = _lat_attn(q, k, v, k_ctx, v_ctx, _window_cols(rpb[l]))

    wqt = w_query[l].T.astype(BF16)
    sk = sub_keys[l].reshape(2 * PEER_HEADS, N_KEYS, PEER_KEY_DIM // 2).astype(BF16)
    x1, u2, st = _merge(x_ctx, x_lat, mods3, h, bg, y_ctx, y_lat, gc, gn, conv_w[l],
                        w_conv_out[l].astype(BF16), w_na_out[l].astype(BF16), w_o[l].astype(BF16),
                        ln1_g[l].reshape(1, D_MODEL), ln1_b[l].reshape(1, D_MODEL), wqt, sk)

    tau, a_sorted, m1, zinv = _select(st)
    peer_args = (u2, st, tau, a_sorted, m1, zinv, expert_u[l].astype(BF16), expert_v[l].T.astype(BF16),
                 x1, mods3, ln2_g[l].reshape(1, D_MODEL), ln2_b[l].reshape(1, D_MODEL))
    y_prompt = _peer(*peer_args, latent=False).reshape(BATCH, SEQ, D_MODEL)
    y_sample = _peer(*peer_args, latent=True).reshape(DEC_BATCH, DEC_SEQ, D_MODEL)
    new_k = k[:N_CTX].reshape(BATCH, 1, SEQ, NA_HEADS, NA_HEAD_DIM)
    new_v = v[:N_CTX].reshape(BATCH, 1, SEQ, NA_HEADS, NA_HEAD_DIM)
    return (y_prompt, y_sample, new_k, new_v)
```

```python
import functools

import numpy as np
import jax
import jax.numpy as jnp
from jax import lax
from jax.experimental import pallas as pl
from jax.experimental.pallas import tpu as pltpu

F32 = jnp.float32
BF16 = jnp.bfloat16

D_MODEL = 1024
BATCH = 16
SEQ = 256
DEC_BATCH = 8
DEC_SEQ = 1024
PAST_LEN = 512
GRID_W = 64
GRID_ROWS = DEC_SEQ // GRID_W
CONV_WIDTH = 512
CONV_K = 3
NA_HEADS = 8
NA_HEAD_DIM = 64
NA_WIDTH = NA_HEADS * NA_HEAD_DIM
WIN_H = 8
WIN_W = 16
PEER_HEADS = 8
PEER_KEY_DIM = 256
N_KEYS = 128
N_EXPERTS = N_KEYS * N_KEYS
PEER_TOPK = 16
LN_EPS = 1e-5
NEG_INF = -1e30
DEPTH = 1
ALPHA = (2 * DEPTH) ** 0.25
IN_COLS = 3 * CONV_WIDTH + 3 * NA_WIDTH + 2 * D_MODEL

N_CTX = BATCH * SEQ
N_LAT = DEC_BATCH * DEC_SEQ
N_TOK = N_CTX + N_LAT
N_COND = 16

ROW_TILE = 256
CTX_TILES = N_CTX // ROW_TILE
LAT_TILES_PER_SEQ = DEC_SEQ // ROW_TILE
N_TILES = N_TOK // ROW_TILE

PEER_TOK = 512
PEER_LANE = 256
PEER_EB = 2048
PEER_ROWS_PER_EB = PEER_EB // N_KEYS
SEL_TOK = 256
LAT_QB = 256
LAT_KROWS = WIN_H + LAT_QB // GRID_W

VMEM_LIMIT = 56 * 1024 * 1024

N_RANK = PEER_TOPK + 1
STAIR = [(a, b) for a in range(N_RANK) for b in range(N_RANK) if (a + 1) * (b + 1) <= N_RANK]


def _cparams(sem):
    return pltpu.CompilerParams(dimension_semantics=sem, vmem_limit_bytes=VMEM_LIMIT)


def _tile_mod_row(i):
    return jnp.where(i < CTX_TILES, 0, 1 + (i - CTX_TILES) // LAT_TILES_PER_SEQ)


def _mods_kernel(cond_ref, w_ref, b_ref, o_ref):
    cnd = cond_ref[...]
    act = cnd * jax.nn.sigmoid(cnd)
    o_ref[...] = jnp.dot(act, w_ref[...], precision=lax.Precision.HIGHEST,
                         preferred_element_type=F32) + b_ref[...]


def _mods(cond, w_ada, b_ada):
    cols = 6 * D_MODEL
    blk = 1536
    return pl.pallas_call(
        _mods_kernel,
        grid=(cols // blk,),
        in_specs=[pl.BlockSpec((N_COND, D_MODEL), lambda j: (0, 0)),
                  pl.BlockSpec((D_MODEL, blk), lambda j: (0, j)),
                  pl.BlockSpec((1, blk), lambda j: (0, j))],
        out_specs=pl.BlockSpec((N_COND, blk), lambda j: (0, j)),
        out_shape=jax.ShapeDtypeStruct((N_COND, cols), F32),
        compiler_params=_cparams(("arbitrary",)),
        name="mods",
    )(cond, w_ada, b_ada.reshape(1, cols))


def _ctx_rows(width):
    return pl.BlockSpec((ROW_TILE, width), lambda i: (jnp.minimum(i, CTX_TILES - 1), 0))


def _lat_rows(width):
    return pl.BlockSpec((ROW_TILE, width), lambda i: (jnp.maximum(i - CTX_TILES, 0), 0))


def _tile_of(ctx_ref, lat_ref):
    return jnp.where(pl.program_id(0) < CTX_TILES, ctx_ref[...], lat_ref[...])


def _inproj_kernel(xc_ref, xl_ref, mod_ref, w_ref, h_ref, bg_ref, q_ref, k_ref, v_ref, gc_ref, gn_ref):
    sh1 = mod_ref[0, :, 0:D_MODEL]
    sc1 = mod_ref[0, :, D_MODEL:2 * D_MODEL]
    u = (_tile_of(xc_ref, xl_ref) * (1 + sc1) + sh1).astype(BF16)

    def proj(lo, hi):
        return jnp.dot(u, w_ref[:, lo:hi], preferred_element_type=F32)

    c = CONV_WIDTH
    a = NA_WIDTH
    xc = proj(0, c)
    cg = proj(2 * c, 3 * c)
    h_ref[...] = cg * xc
    bg_ref[...] = proj(c, 2 * c)
    o = 3 * c
    q_ref[...] = (proj(o, o + a) * (NA_HEAD_DIM ** -0.5)).astype(BF16)
    k_ref[...] = proj(o + a, o + 2 * a)
    v_ref[...] = proj(o + 2 * a, o + 3 * a)
    o = o + 3 * a
    gc_ref[...] = jax.nn.sigmoid(proj(o, o + D_MODEL))
    gn_ref[...] = jax.nn.sigmoid(proj(o + D_MODEL, o + 2 * D_MODEL))


def _inproj(x_ctx, x_lat, mods3, w_in_bf):
    row = lambda w: pl.BlockSpec((ROW_TILE, w), lambda i: (i, 0))
    return pl.pallas_call(
        _inproj_kernel,
        grid=(N_TILES,),
        in_specs=[_ctx_rows(D_MODEL), _lat_rows(D_MODEL),
                  pl.BlockSpec((1, 1, 6 * D_MODEL), lambda i: (_tile_mod_row(i), 0, 0)),
                  pl.BlockSpec((D_MODEL, IN_COLS), lambda i: (0, 0))],
        out_specs=[row(CONV_WIDTH), row(CONV_WIDTH), row(NA_WIDTH), row(NA_WIDTH), row(NA_WIDTH),
                   row(D_MODEL), row(D_MODEL)],
        out_shape=[jax.ShapeDtypeStruct((N_TOK, CONV_WIDTH), F32),
                   jax.ShapeDtypeStruct((N_TOK, CONV_WIDTH), F32),
                   jax.ShapeDtypeStruct((N_TOK, NA_WIDTH), BF16),
                   jax.ShapeDtypeStruct((N_TOK, NA_WIDTH), F32),
                   jax.ShapeDtypeStruct((N_TOK, NA_WIDTH), F32),
                   jax.ShapeDtypeStruct((N_TOK, D_MODEL), F32),
                   jax.ShapeDtypeStruct((N_TOK, D_MODEL), F32)],
        compiler_params=_cparams(("parallel",)),
        name="inproj",
    )(x_ctx, x_lat, mods3, w_in_bf)


def _head_pair_attention(q2, kv_list, bias_list):
    lane = lax.broadcasted_iota(jnp.int32, q2.shape, 1)
    outs = []
    for hh in range(2):
        in_head = (lane >= hh * NA_HEAD_DIM) & (lane < (hh + 1) * NA_HEAD_DIM)
        qh = jnp.where(in_head, q2, jnp.zeros_like(q2))
        scores = []
        for (k2, _), bias in zip(kv_list, bias_list):
            s = lax.dot_general(qh, k2, (((1,), (1,)), ((), ())), preferred_element_type=F32)
            if bias is not None:
                s = s + bias(hh)
            scores.append(s)
        m = scores[0].max(axis=-1, keepdims=True)
        for s in scores[1:]:
            m = jnp.maximum(m, s.max(axis=-1, keepdims=True))
        acc = None
        den = None
        for s, (_, v2) in zip(scores, kv_list):
            p = jnp.exp(s - m)
            d = p.sum(axis=-1, keepdims=True)
            o = jnp.dot(p.astype(BF16), v2, preferred_element_type=F32)
            acc = o if acc is None else acc + o
            den = d if den is None else den + d
        outs.append(acc / den)
    return jnp.where(lane < NA_HEAD_DIM, outs[0], outs[1])


def _ctx_attn_kernel(q_ref, k_ref, v_ref, o_ref):
    for p in range(NA_WIDTH // 128):
        lanes = slice(p * 128, (p + 1) * 128)
        k2 = k_ref[:, lanes].astype(BF16)
        v2 = v_ref[:, lanes].astype(BF16)
        o_ref[:, lanes] = _head_pair_attention(q_ref[:, lanes], [(k2, v2)], [None]).astype(BF16)


def _ctx_attn(q, k, v):
    spec = pl.BlockSpec((SEQ, NA_WIDTH), lambda b: (b, 0))
    return pl.pallas_call(
        _ctx_attn_kernel,
        grid=(BATCH,),
        in_specs=[spec, spec, spec],
        out_specs=spec,
        out_shape=jax.ShapeDtypeStruct((N_CTX, NA_WIDTH), BF16),
        compiler_params=_cparams(("parallel",)),
        name="ctx_attn",
    )(q, k, v)


def _lat_attn_kernel(q_ref, k_ref, v_ref, kc_ref, vc_ref, cols_ref, o_ref, bias_ref):
    n_pairs = NA_WIDTH // 128
    kh = min(WIN_H, GRID_ROWS)
    q_rows = LAT_QB // GRID_W
    key_row0 = jnp.clip(pl.program_id(0) * q_rows - kh // 2, 0, GRID_ROWS - LAT_KROWS)

    @pl.when(pl.program_id(1) == 0)
    def _build_bias():
        for lr in range(q_rows):
            qr = pl.program_id(0) * q_rows + lr
            row_start = jnp.clip(qr - kh // 2, 0, GRID_ROWS - kh)
            for kl in range(LAT_KROWS):
                kr = key_row0 + kl
                in_rows = (kr >= row_start) & (kr < row_start + kh)
                dr = jnp.clip(kr - qr + (WIN_H - 1), 0, 2 * WIN_H - 2)
                for hd in range(NA_HEADS):
                    tile = jnp.where(in_rows, cols_ref[hd, dr], NEG_INF)
                    bias_ref[hd, lr * GRID_W:(lr + 1) * GRID_W, kl * GRID_W:(kl + 1) * GRID_W] = tile

    keys = pl.ds(pl.multiple_of(key_row0 * GRID_W, GRID_W), LAT_KROWS * GRID_W)
    for p in range(n_pairs):
        lanes = slice(p * 128, (p + 1) * 128)
        kw = k_ref[keys, lanes].astype(BF16)
        vw = v_ref[keys, lanes].astype(BF16)
        kc = kc_ref[0, :, lanes].astype(BF16)
        vc = vc_ref[0, :, lanes].astype(BF16)
        o = _head_pair_attention(q_ref[:, lanes], [(kw, vw), (kc, vc)],
                                 [lambda hh, p=p: bias_ref[2 * p + hh], None])
        o_ref[:, lanes] = o.astype(BF16)


def _lat_attn(q, k, v, k_ctx, v_ctx, cols):
    n_qb = DEC_SEQ // LAT_QB
    q_off = N_CTX // LAT_QB
    kv_off = N_CTX // DEC_SEQ
    return pl.pallas_call(
        _lat_attn_kernel,
        grid=(n_qb, DEC_BATCH),
        in_specs=[pl.BlockSpec((LAT_QB, NA_WIDTH), lambda qb, b: (q_off + b * n_qb + qb, 0)),
                  pl.BlockSpec((DEC_SEQ, NA_WIDTH), lambda qb, b: (kv_off + b, 0)),
                  pl.BlockSpec((DEC_SEQ, NA_WIDTH), lambda qb, b: (kv_off + b, 0)),
                  pl.BlockSpec((1, PAST_LEN, NA_WIDTH), lambda qb, b: (b, 0, 0)),
                  pl.BlockSpec((1, PAST_LEN, NA_WIDTH), lambda qb, b: (b, 0, 0)),
                  pl.BlockSpec((NA_HEADS, 2 * WIN_H - 1, GRID_W, GRID_W), lambda qb, b: (0, 0, 0, 0))],
        out_specs=pl.BlockSpec((LAT_QB, NA_WIDTH), lambda qb, b: (b * n_qb + qb, 0)),
        out_shape=jax.ShapeDtypeStruct((N_LAT, NA_WIDTH), BF16),
        scratch_shapes=[pltpu.VMEM((NA_HEADS, LAT_QB, LAT_KROWS * GRID_W), F32)],
        compiler_params=_cparams(("parallel", "arbitrary")),
        name="lat_attn",
    )(q, k, v, k_ctx, v_ctx, cols)


def _window_cols(rpb):
    col = np.arange(GRID_W)
    col_start = np.clip(col - WIN_W // 2, 0, GRID_W - WIN_W)
    col_in = (col[None, :] >= col_start[:, None]) & (col[None, :] < col_start[:, None] + WIN_W)
    dc = np.clip(col[None, :] - col[:, None], -(WIN_W - 1), WIN_W - 1) + (WIN_W - 1)
    onehot = (np.arange(2 * WIN_W - 1)[:, None, None] == dc[None]).astype(np.float32)
    cols = jnp.einsum('hrd,dqk->hrqk', rpb.astype(F32), jnp.asarray(onehot), precision=lax.Precision.HIGHEST)
    return jnp.where(jnp.asarray(col_in)[None, None], cols, NEG_INF)


def _layer_norm(x, g, b):
    mu = jnp.mean(x, axis=-1, keepdims=True)
    xc = x - mu
    var = jnp.mean(xc * xc, axis=-1, keepdims=True)
    return xc * lax.rsqrt(var + LN_EPS) * g + b


def _merge_kernel(xc_ref, xl_ref, mod_ref, h_ref, hp_ref, hn_ref, bg_ref, ync_ref, ynl_ref, gc_ref, gn_ref,
                  cw_ref, wco_ref, wno_ref, wo_ref, g1_ref, b1_ref, wqt_ref, sk_ref,
                  x1_ref, u2_ref, st_ref, tau_ref, a_ref, m1_ref, zinv_ref, top_ref):
    i = pl.program_id(0)
    pos = (i - CTX_TILES) % LAT_TILES_PER_SEQ
    is_ctx = i < CTX_TILES
    has_prev = jnp.where(is_ctx | (pos == 0), 0.0, 1.0).astype(F32)
    has_next = jnp.where(is_ctx | (pos == LAT_TILES_PER_SEQ - 1), 0.0, 1.0).astype(F32)

    h = h_ref[...]
    rows = lax.broadcasted_iota(jnp.int32, h.shape, 0)
    prev_row = hp_ref[7:8, :] * has_prev
    next_row = hn_ref[0:1, :] * has_next
    h_prev = jnp.where(rows == 0, prev_row, pltpu.roll(h, 1, 0))
    h_next = jnp.where(rows == ROW_TILE - 1, next_row, pltpu.roll(h, ROW_TILE - 1, 0))
    y = cw_ref[0:1, :] * h_prev
    y = y + cw_ref[1:2, :] * h
    y = y + cw_ref[2:3, :] * h_next
    y_conv = (bg_ref[...] * y).astype(BF16)

    merged = (gc_ref[...] * jnp.dot(y_conv, wco_ref[...], preferred_element_type=F32) +
              gn_ref[...] * jnp.dot(_tile_of(ync_ref, ynl_ref), wno_ref[...],
                                    preferred_element_type=F32))
    z = jnp.dot(merged.astype(BF16), wo_ref[...], preferred_element_type=F32)

    ga1 = mod_ref[0, :, 2 * D_MODEL:3 * D_MODEL]
    sh2 = mod_ref[0, :, 3 * D_MODEL:4 * D_MODEL]
    sc2 = mod_ref[0, :, 4 * D_MODEL:5 * D_MODEL]
    x1 = _layer_norm(ALPHA * _tile_of(xc_ref, xl_ref) + ga1 * z, g1_ref[...], b1_ref[...])
    x1_ref[...] = x1
    u2 = (x1 * (1 + sc2) + sh2).astype(BF16)
    u2_ref[...] = u2

    qt = lax.dot_general(wqt_ref[...], u2, (((1,), (1,)), ((), ())), preferred_element_type=F32)
    half = PEER_KEY_DIM // 2
    for hp in range(2 * PEER_HEADS):
        blk = qt[hp * half:(hp + 1) * half, :].astype(BF16)
        st_ref[hp * N_KEYS:(hp + 1) * N_KEYS, :] = jnp.dot(sk_ref[hp], blk, preferred_element_type=F32)

    _select_kernel(st_ref, tau_ref, a_ref, m1_ref, zinv_ref, top_ref)


def _merge(x_ctx, x_lat, mods3, h, bg, y_ctx, y_lat, gc, gn, conv_w, wco, wno, wo, g1, b1, wqt, sk):
    row = lambda w: pl.BlockSpec((ROW_TILE, w), lambda i: (i, 0))
    full = lambda *s: pl.BlockSpec(s, lambda i: (0,) * len(s))
    eights = ROW_TILE // 8
    n8 = N_TOK // 8
    stat = pl.BlockSpec((PEER_HEADS, SEL_TOK), lambda i: (0, i))
    shp = jax.ShapeDtypeStruct((PEER_HEADS, N_TOK), F32)
    return pl.pallas_call(
        _merge_kernel,
        grid=(N_TILES,),
        in_specs=[_ctx_rows(D_MODEL), _lat_rows(D_MODEL),
                  pl.BlockSpec((1, 1, 6 * D_MODEL), lambda i: (_tile_mod_row(i), 0, 0)),
                  row(CONV_WIDTH),
                  pl.BlockSpec((8, CONV_WIDTH), lambda i: (jnp.maximum(i * eights - 1, 0), 0)),
                  pl.BlockSpec((8, CONV_WIDTH), lambda i: (jnp.minimum((i + 1) * eights, n8 - 1), 0)),
                  row(CONV_WIDTH), _ctx_rows(NA_WIDTH), _lat_rows(NA_WIDTH), row(D_MODEL), row(D_MODEL),
                  full(CONV_K, CONV_WIDTH), full(CONV_WIDTH, D_MODEL), full(NA_WIDTH, D_MODEL),
                  full(D_MODEL, D_MODEL), full(1, D_MODEL), full(1, D_MODEL),
                  full(PEER_HEADS * PEER_KEY_DIM, D_MODEL),
                  full(2 * PEER_HEADS, N_KEYS, PEER_KEY_DIM // 2)],
        out_specs=[row(D_MODEL), row(D_MODEL),
                   pl.BlockSpec((2 * PEER_HEADS * N_KEYS, ROW_TILE), lambda i: (0, i)),
                   stat, pl.BlockSpec((PEER_TOPK, PEER_HEADS, SEL_TOK), lambda i: (0, 0, i)), stat, stat],
        out_shape=[jax.ShapeDtypeStruct((N_TOK, D_MODEL), F32),
                   jax.ShapeDtypeStruct((N_TOK, D_MODEL), BF16),
                   jax.ShapeDtypeStruct((2 * PEER_HEADS * N_KEYS, N_TOK), F32),
                   shp, jax.ShapeDtypeStruct((PEER_TOPK, PEER_HEADS, N_TOK), F32), shp, shp],
        scratch_shapes=[pltpu.VMEM((2 * PEER_HEADS, N_RANK, 8, SEL_TOK), F32)],
        compiler_params=_cparams(("parallel",)),
        name="merge",
    )(x_ctx, x_lat, mods3, h, h, h, bg, y_ctx, y_lat, gc, gn, conv_w, wco, wno, wo, g1, b1, wqt, sk)


def _sort16_desc(v):
    v = list(v)
    n = len(v)
    k = 2
    while k <= n:
        j = k // 2
        while j >= 1:
            for i in range(n):
                m = i ^ j
                if m > i:
                    hi, lo = jnp.maximum(v[i], v[m]), jnp.minimum(v[i], v[m])
                    v[i], v[m] = (hi, lo) if (i & k) == 0 else (lo, hi)
            j //= 2
        k *= 2
    return v


def _merge_top16(x, y):
    n = len(x)
    keep = [jnp.maximum(x[i], y[n - 1 - i]) for i in range(n)]
    drop = [jnp.minimum(x[i], y[n - 1 - i]) for i in range(n)]
    while len(drop) > 1:
        drop = [jnp.maximum(drop[2 * i], drop[2 * i + 1]) for i in range(len(drop) // 2)]
    j = n // 2
    while j >= 1:
        for i in range(n):
            m = i ^ j
            if m > i:
                keep[i], keep[m] = jnp.maximum(keep[i], keep[m]), jnp.minimum(keep[i], keep[m])
        j //= 2
    return keep, drop[0]


def _select_kernel(st_ref, tau_ref, a_ref, m1_ref, zinv_ref, top_ref):
    n_chunks = N_KEYS // 8

    def one_list(hp, carry):
        base = pl.multiple_of(hp * N_KEYS, N_KEYS)
        cur = _sort16_desc([st_ref[pl.ds(base + 8 * c, 8), :] for c in range(n_chunks)])
        rest = None
        for shift in (4, 2, 1):
            cur, dropped = _merge_top16(cur, [pltpu.roll(x, shift, 0) for x in cur])
            if rest is not None:
                dropped = jnp.maximum(dropped, jnp.maximum(rest, pltpu.roll(rest, shift, 0)))
            rest = dropped
        for r in range(PEER_TOPK):
            top_ref[hp, r] = cur[r]
        top_ref[hp, PEER_TOPK] = rest
        return carry

    lax.fori_loop(0, 2 * PEER_HEADS, one_list, 0)

    sub = lax.broadcasted_iota(jnp.int32, (PEER_HEADS, SEL_TOK), 0)

    def packed(p, r):
        out = top_ref[p, r]
        for hd in range(1, PEER_HEADS):
            out = jnp.where(sub == hd, top_ref[2 * hd + p, r], out)
        return out

    a = [packed(0, r) for r in range(N_RANK)]
    b = [packed(1, r) for r in range(N_RANK)]
    cand = [a[r0] + b[r1] for (r0, r1) in STAIR]
    cand = cand + [jnp.full_like(cand[0], -jnp.inf)] * (-len(cand) % PEER_TOPK)
    blocks = [_sort16_desc(cand[i:i + PEER_TOPK]) for i in range(0, len(cand), PEER_TOPK)]
    rests = []
    while len(blocks) > 1:
        merged = []
        for i in range(0, len(blocks) - 1, 2):
            keep, dropped = _merge_top16(blocks[i], blocks[i + 1])
            merged.append(keep)
            rests.append(dropped)
        blocks = merged + blocks[len(blocks) - len(blocks) % 2:]
    top = blocks[0]
    c17 = rests[0]
    for x in rests[1:]:
        c17 = jnp.maximum(c17, x)
    z = jnp.ones_like(top[0])
    for x in top[1:]:
        z = z + jnp.exp(x - top[0])
    tau_ref[...] = 0.5 * (top[PEER_TOPK - 1] + c17)
    for r in range(PEER_TOPK):
        a_ref[r] = a[r]
    m1_ref[...] = b[0]
    zinv_ref[...] = 1.0 / z


def _select(st):
    n_tok = st.shape[1]
    stat = pl.BlockSpec((PEER_HEADS, SEL_TOK), lambda i: (0, i))
    shp = jax.ShapeDtypeStruct((PEER_HEADS, n_tok), F32)
    return pl.pallas_call(
        _select_kernel,
        grid=(n_tok // SEL_TOK,),
        in_specs=[pl.BlockSpec((2 * PEER_HEADS * N_KEYS, SEL_TOK), lambda i: (0, i))],
        out_specs=[stat, pl.BlockSpec((PEER_TOPK, PEER_HEADS, SEL_TOK), lambda i: (0, 0, i)), stat, stat],
        out_shape=[shp, jax.ShapeDtypeStruct((PEER_TOPK, PEER_HEADS, n_tok), F32), shp, shp],
        scratch_shapes=[pltpu.VMEM((2 * PEER_HEADS, N_RANK, 8, SEL_TOK), F32)],
        compiler_params=_cparams(("parallel",)),
        name="select",
    )(st)


def _row_bf16(row):
    return jnp.broadcast_to(row, (16, row.shape[1])).astype(BF16)


def _twice_gelu(x):
    return x + x * lax.erf(x * np.float32(np.sqrt(0.5)))


def _peer_kernel(u2_ref, st_ref, tau_ref, a_ref, m1_ref, zinv_ref, u_ref, v_ref,
                 x1_ref, mod_ref, g2_ref, b2_ref, o_ref,
                 rank0_ref, e0_ref, adm1_ref, e1_ref, s_ref, p_ref, acc_ref):
    eb = pl.program_id(1)

    @pl.when(eb == 0)
    def _prologue():
        half = N_KEYS // 2
        for hd in range(PEER_HEADS):
            for lane0 in range(0, PEER_TOK, 128):
                lanes = slice(lane0, lane0 + 128)
                a = [a_ref[r, hd:hd + 1, lanes] for r in range(PEER_TOPK)]
                tau = tau_ref[hd:hd + 1, lanes]
                for k0 in (0, half):
                    s0 = st_ref[(2 * hd) * N_KEYS + k0:(2 * hd) * N_KEYS + k0 + half, lanes]
                    s1 = st_ref[(2 * hd + 1) * N_KEYS + k0:(2 * hd + 1) * N_KEYS + k0 + half, lanes]
                    need = tau - s1
                    rank0 = jnp.zeros_like(s0)
                    adm1 = jnp.zeros_like(s1)
                    for r in range(PEER_TOPK):
                        rank0 = jnp.where(a[r] > s0, float(r + 1), rank0)
                        adm1 = jnp.where(a[r] >= need, float(r + 1), adm1)
                    rows = slice(hd * N_KEYS + k0, hd * N_KEYS + k0 + half)
                    rank0_ref[rows, lanes] = rank0
                    adm1_ref[rows, lanes] = adm1.astype(BF16)
                    e0_ref[rows, lanes] = jnp.exp(s0 - a[0]) * (0.5 * zinv_ref[hd:hd + 1, lanes])
                    e1_ref[rows, lanes] = jnp.exp(s1 - m1_ref[hd:hd + 1, lanes]).astype(BF16)
        acc_ref[...] = jnp.zeros_like(acc_ref)

    s_ref[...] = lax.dot_general(u_ref[...], u2_ref[...], (((1,), (1,)), ((), ())),
                                 preferred_element_type=F32)

    def gate_rows(r, carry):
        rows = pl.ds(pl.multiple_of(r * N_KEYS, N_KEYS), N_KEYS)
        key0 = eb * PEER_ROWS_PER_EB + r
        tile = (N_KEYS // 16, 16, PEER_LANE)
        for lane0 in range(0, PEER_TOK, PEER_LANE):
            lanes = slice(lane0, lane0 + PEER_LANE)
            w = jnp.zeros(tile, BF16)
            for hd in range(PEER_HEADS):
                adm1 = adm1_ref[hd * N_KEYS:(hd + 1) * N_KEYS, lanes].reshape(tile)
                e1 = e1_ref[hd * N_KEYS:(hd + 1) * N_KEYS, lanes].reshape(tile)
                rank0 = _row_bf16(rank0_ref[pl.ds(hd * N_KEYS + key0, 1), lanes])[None]
                e0 = _row_bf16(e0_ref[pl.ds(hd * N_KEYS + key0, 1), lanes])[None]
                w = w + jnp.where(rank0 < adm1, e1, jnp.zeros_like(e1)) * e0
            g2 = _twice_gelu(s_ref[rows, lanes]).astype(BF16)
            p_ref[rows, lanes] = w.reshape(N_KEYS, PEER_LANE) * g2
        return carry

    lax.fori_loop(0, PEER_ROWS_PER_EB, gate_rows, 0)

    acc_ref[...] += lax.dot_general(p_ref[...], v_ref[...], (((0,), (0,)), ((), ())),
                                    preferred_element_type=F32)

    @pl.when(eb == pl.num_programs(1) - 1)
    def _epilogue():
        y = acc_ref[...]
        ga2 = mod_ref[0, :, 5 * D_MODEL:6 * D_MODEL]
        o_ref[...] = _layer_norm(ALPHA * x1_ref[...] + ga2 * y, g2_ref[...], b2_ref[...])


def _peer(u2, st, tau, a_sorted, m1, zinv, u_bf, v_bf, x1, mods3, g2, b2, latent):
    n_tok = N_LAT if latent else N_CTX
    t0 = (N_CTX if latent else 0) // PEER_TOK
    tiles_per_seq = DEC_SEQ // PEER_TOK
    mod_row = (lambda t: 1 + t // tiles_per_seq) if latent else (lambda t: 0)
    stat = pl.BlockSpec((PEER_HEADS, PEER_TOK), lambda t, e: (0, t0 + t))
    return pl.pallas_call(
        _peer_kernel,
        grid=(n_tok // PEER_TOK, N_EXPERTS // PEER_EB),
        in_specs=[pl.BlockSpec((PEER_TOK, D_MODEL), lambda t, e: (t0 + t, 0)),
                  pl.BlockSpec((2 * PEER_HEADS * N_KEYS, PEER_TOK), lambda t, e: (0, t0 + t)),
                  stat,
                  pl.BlockSpec((PEER_TOPK, PEER_HEADS, PEER_TOK), lambda t, e: (0, 0, t0 + t)),
                  stat, stat,
                  pl.BlockSpec((PEER_EB, D_MODEL), lambda t, e: (e, 0)),
                  pl.BlockSpec((PEER_EB, D_MODEL), lambda t, e: (e, 0)),
                  pl.BlockSpec((PEER_TOK, D_MODEL), lambda t, e: (t0 + t, 0)),
                  pl.BlockSpec((1, 1, 6 * D_MODEL), lambda t, e: (mod_row(t), 0, 0)),
                  pl.BlockSpec((1, D_MODEL), lambda t, e: (0, 0)),
                  pl.BlockSpec((1, D_MODEL), lambda t, e: (0, 0))],
        out_specs=pl.BlockSpec((PEER_TOK, D_MODEL), lambda t, e: (t, 0)),
        out_shape=jax.ShapeDtypeStruct((n_tok, D_MODEL), F32),
        scratch_shapes=[pltpu.VMEM((PEER_HEADS * N_KEYS, PEER_TOK), F32),
                        pltpu.VMEM((PEER_HEADS * N_KEYS, PEER_TOK), F32),
                        pltpu.VMEM((PEER_HEADS * N_KEYS, PEER_TOK), BF16),
                        pltpu.VMEM((PEER_HEADS * N_KEYS, PEER_TOK), BF16),
                        pltpu.VMEM((PEER_EB, PEER_TOK), F32),
                        pltpu.VMEM((PEER_EB, PEER_TOK), BF16),
                        pltpu.VMEM((PEER_TOK, D_MODEL), F32)],
        compiler_params=_cparams(("parallel", "arbitrary")),
        name="peer",
    )(u2, st, tau, a_sorted, m1, zinv, u_bf, v_bf, x1, mods3, g2, b2)


def kernel(x_prompt, x_sample, cache_k, cache_v, c, c_ctx, w_ada, b_ada, w_in, conv_w, w_conv_out,
           w_na_out, rpb, w_o, ln1_g, ln1_b, w_query, sub_keys, expert_u, expert_v, ln2_g, ln2_b):
    l = 0
    x_ctx = x_prompt.reshape(N_CTX, D_MODEL)
    x_lat = x_sample.reshape(N_LAT, D_MODEL)
    cond = jnp.concatenate([c_ctx[None, :], c, jnp.zeros((N_COND - 1 - DEC_BATCH, D_MODEL), F32)], axis=0)

    mods = _mods(cond, w_ada[l], b_ada[l])
    mods3 = mods.reshape(N_COND, 1, 6 * D_MODEL)

    h, bg, q, k, v, gc, gn = _inproj(x_ctx, x_lat, mods3, w_in[l].astype(BF16))

    y_ctx = _ctx_attn(q, k, v)
    k_ctx = cache_k[:, l].reshape(DEC_BATCH, PAST_LEN, NA_WIDTH)
    v_ctx = cache_v[:, l].reshape(DEC_BATCH, PAST_LEN, NA_WIDTH)
    y_lat = _lat_attn(q, k, v, k_ctx, v_ctx, _window_cols(rpb[l]))

    wqt = w_query[l].T.astype(BF16)
    sk = sub_keys[l].reshape(2 * PEER_HEADS, N_KEYS, PEER_KEY_DIM // 2).astype(BF16)
    x1, u2, st, tau, a_sorted, m1, zinv = _merge(x_ctx, x_lat, mods3, h, bg, y_ctx, y_lat, gc, gn, conv_w[l],
                        w_conv_out[l].astype(BF16), w_na_out[l].astype(BF16), w_o[l].astype(BF16),
                        ln1_g[l].reshape(1, D_MODEL), ln1_b[l].reshape(1, D_MODEL), wqt, sk)

    peer_args = (u2, st, tau, a_sorted, m1, zinv, expert_u[l].astype(BF16), expert_v[l].astype(BF16),
                 x1, mods3, ln2_g[l].reshape(1, D_MODEL), ln2_b[l].reshape(1, D_MODEL))
    y_prompt = _peer(*peer_args, latent=False).reshape(BATCH, SEQ, D_MODEL)
    y_sample = _peer(*peer_args, latent=True).reshape(DEC_BATCH, DEC_SEQ, D_MODEL)
    new_k = k[:N_CTX].reshape(BATCH, 1, SEQ, NA_HEADS, NA_HEAD_DIM)
    new_v = v[:N_CTX].reshape(BATCH, 1, SEQ, NA_HEADS, NA_HEAD_DIM)
    return (y_prompt, y_sample, new_k, new_v)
```
